```python
import math
import jax
import jax.numpy as jnp
from jax import lax
import numpy as np

D_MODEL = 1024
BATCH = 8
SEQ = 2048
DEPTH = 4
DEC_BATCH = 128
DEC_SEQ = 8
PAST_LEN = 16384
PAGE_SIZE = 128

N_META = 16
CONV_W = 4
LN_EPS = 1e-5
RMS_EPS = 1e-6
DN_HEADS = 4
DN_DK = 128
DN_DV = 128
DN_QK = DN_HEADS * DN_DK
DN_V = DN_HEADS * DN_DV
DN_CONV_CH = 2 * DN_QK + DN_V
DN_CHUNK = 64
LRU_WIDTH = D_MODEL // 2
LRU_BLOCKS = 8
LRU_BD = LRU_WIDTH // LRU_BLOCKS
LRU_C = 8.0
SSD_INNER = 2 * D_MODEL
SSD_HEADDIM = 64
SSD_HEADS = SSD_INNER // SSD_HEADDIM
SSD_GROUPS = 8
SSD_HPG = SSD_HEADS // SSD_GROUPS
SSD_STATE = 128
SSD_CHUNK = 128
SSD_BC = SSD_GROUPS * SSD_STATE
SSD_CONV_CH = SSD_INNER + 2 * SSD_BC
MOE_GROUPS = 4
MOE_EPG = 8
MOE_HIDDEN = D_MODEL // 4
MOE_TOPK = 2
N_AB = (DEPTH + 1) // 2
N_SSD = DEPTH // 2
AB_SPLIT = (DN_QK, DN_QK, DN_V, DN_V, DN_HEADS, DN_HEADS, LRU_WIDTH, LRU_WIDTH)
AB_IN = sum(AB_SPLIT)
AB_MIX = DN_V + LRU_WIDTH
SSD_SPLIT = (SSD_INNER, SSD_CONV_CH, SSD_HEADS)
SSD_IN = sum(SSD_SPLIT)
ALPHA = (2.0 * DEPTH) ** 0.25
BETA = (8.0 * DEPTH) ** -0.25

kernel_name = 'hybrid_deltanet_rglru_ssd_hmoe_step'


def split_last(x, sizes):
    idx = [int(s) for s in np.cumsum(sizes)[:-1]]
    return jnp.split(x, idx, axis=-1)


def layer_norm(x, g, b):
    xf = x.astype(jnp.float32)
    mu = jnp.mean(xf, axis=-1, keepdims=True)
    var = jnp.mean(jnp.square(xf - mu), axis=-1, keepdims=True)
    y = (xf - mu) * lax.rsqrt(var + LN_EPS) * g.astype(jnp.float32) + b.astype(jnp.float32)
    return y.astype(x.dtype)


def rms_normalize(x):
    return x * lax.rsqrt(jnp.mean(jnp.square(x), axis=-1, keepdims=True) + RMS_EPS)


def l2_normalize(x):
    return x * lax.rsqrt(jnp.sum(jnp.square(x), axis=-1, keepdims=True) + RMS_EPS)


def conv_with_history(x, hist, w):
    x_ext = jnp.concatenate([hist.astype(x.dtype), x], axis=1)
    length = x.shape[1]
    y = x_ext[:, 0:length] * w[0]
    for k in range(1, CONV_W):
        y = y + x_ext[:, k:k + length] * w[k]
    return y, x_ext[:, length:]


def chunk_segments(length, chunk, prompt):
    if prompt:
        return ((N_META, N_META), (length - N_META, chunk))
    return ((length, math.gcd(length, chunk)),)


def scan_chunks(step, carry, xs, segments):
    outs = []
    start = 0
    for length, chunk in segments:
        n = length // chunk
        seg = tuple(
            jnp.moveaxis(lax.slice_in_dim(a, start, start + length, axis=1).reshape(a.shape[0], n, chunk, *a.shape[2:]), 1, 0)
            for a in xs)
        carry, y = lax.scan(step, carry, seg)
        y = jnp.moveaxis(y, 0, 1)
        outs.append(y.reshape(y.shape[0], length, *y.shape[3:]))
        start += length
    y_all = jnp.concatenate(outs, axis=1) if len(outs) > 1 else outs[0]
    return y_all, carry


def delta_chunk_step(s, inp):
    q, k, v, g, beta = [jnp.swapaxes(t, 1, 2) for t in inp]
    t = q.shape[2]
    causal = jnp.tril(jnp.ones((t, t), dtype=bool))
    eye = jnp.eye(t, dtype=q.dtype)
    gc = jnp.cumsum(g, axis=-1)
    decay = jnp.exp(jnp.where(causal, gc[..., :, None] - gc[..., None, :], -jnp.inf))
    kb = k * beta[..., None]
    a_mat = jnp.einsum('bhtd,bhsd->bhts', kb, k) * decay * (1.0 - eye)
    rhs = jnp.concatenate([v * beta[..., None], kb * jnp.exp(gc)[..., None]], axis=-1)
    sol = lax.linalg.triangular_solve(a_mat + eye, rhs, left_side=True, lower=True, unit_diagonal=True)
    u, w = sol[..., :DN_DV], sol[..., DN_DV:]
    v_new = u - jnp.einsum('bhtk,bhkv->bhtv', w, s)
    attn = jnp.einsum('bhtk,bhsk->bhts', q, k) * decay
    o = jnp.einsum('bhtk,bhkv->bhtv', q * jnp.exp(gc)[..., None], s) + jnp.einsum('bhts,bhsv->bhtv', attn, v_new)
    g_last = gc[..., -1:]
    s_new = s * jnp.exp(g_last)[..., None] + jnp.einsum('bhsk,bhsv->bhkv', k * jnp.exp(g_last - gc)[..., None], v_new)
    return s_new, jnp.swapaxes(o, 1, 2)


def ssd_chunk_step(s, inp):
    xdt, da, bm, cm = inp
    t = xdt.shape[1]
    causal = jnp.tril(jnp.ones((t, t), dtype=bool))[None, :, :, None, None]
    cs = jnp.cumsum(da, axis=1)
    decay = jnp.exp(jnp.where(causal, cs[:, :, None] - cs[:, None, :], -jnp.inf))
    cb = jnp.einsum('btgn,bsgn->btsg', cm, bm)
    y = jnp.einsum('btsg,btsge,bsgep->btgep', cb, decay, xdt)
    y = y + jnp.einsum('btgn,bgepn->btgep', cm, s) * jnp.exp(cs)[..., None]
    c_last = cs[:, -1:]
    s_new = s * jnp.exp(c_last[:, 0])[..., None, None] + jnp.einsum('bsgn,bsge,bsgep->bgepn', bm, jnp.exp(c_last - cs), xdt)
    return s_new, y


def lru_combine(c1, c2):
    a1, b1 = c1
    a2, b2 = c2
    return a1 * a2, a2 * b1 + b2


def gated_delta_mixer(q, k, v, z, b, a, conv_hist, s0, conv_w, a_log, dt_bias, norm_g, segments):
    bsz, length, _ = q.shape
    f32 = jnp.float32
    qkv, new_hist = conv_with_history(jnp.concatenate([q, k, v], axis=-1), conv_hist, conv_w)
    qkv = jax.nn.silu(qkv.astype(f32))
    q, k, v = split_last(qkv, (DN_QK, DN_QK, DN_V))
    q = l2_normalize(q.reshape(bsz, length, DN_HEADS, DN_DK)) * (DN_DK ** -0.5)
    k = l2_normalize(k.reshape(bsz, length, DN_HEADS, DN_DK))
    v = v.reshape(bsz, length, DN_HEADS, DN_DV)
    beta = jax.nn.sigmoid(b.astype(f32))
    g = -jnp.exp(a_log.astype(f32)) * jax.nn.softplus(a.astype(f32) + dt_bias.astype(f32))
    o, s_new = scan_chunks(delta_chunk_step, s0.astype(f32), (q, k, v, g, beta), segments)
    o = rms_normalize(o) * norm_g.astype(f32) * jax.nn.silu(z.astype(f32).reshape(bsz, length, DN_HEADS, DN_DV))
    return o.reshape(bsz, length, DN_V), new_hist, s_new


def rglru_mixer(xb, yb, conv_hist, h0, conv_w, conv_b, w_a, b_a, w_x, b_x, lam):
    bsz, length, _ = xb.shape
    f32 = jnp.float32
    xc, new_hist = conv_with_history(xb, conv_hist, conv_w)
    xc = (xc + conv_b).astype(f32)
    xh = xc.reshape(bsz, length, LRU_BLOCKS, LRU_BD)
    r = jax.nn.sigmoid(jnp.einsum('blnd,nde->blne', xh, w_a.astype(f32)).reshape(bsz, length, LRU_WIDTH) + b_a.astype(f32))
    i = jax.nn.sigmoid(jnp.einsum('blnd,nde->blne', xh, w_x.astype(f32)).reshape(bsz, length, LRU_WIDTH) + b_x.astype(f32))
    log_a = -LRU_C * r * jax.nn.softplus(-lam.astype(f32))
    a = jnp.exp(log_a)
    u = jnp.sqrt(-jnp.expm1(2.0 * log_a)) * (i * xc)
    u = u.at[:, 0].add(a[:, 0] * h0.astype(f32))
    _, h = lax.associative_scan(lru_combine, (a, u), axis=1)
    y = jax.nn.gelu(yb.astype(f32)) * h
    return y, new_hist, h[:, -1]


def ssd_mixer(proj, conv_hist, s0, conv_w, conv_b, dt_bias, a_log, d_skip, norm_g, segments):
    bsz, length, _ = proj.shape
    f32 = jnp.float32
    z, xbc, dt = split_last(proj, SSD_SPLIT)
    xbc, new_hist = conv_with_history(xbc, conv_hist, conv_w)
    xbc = jax.nn.silu((xbc + conv_b).astype(f32))
    xs, bm, cm = split_last(xbc, (SSD_INNER, SSD_BC, SSD_BC))
    xs = xs.reshape(bsz, length, SSD_GROUPS, SSD_HPG, SSD_HEADDIM)
    bm = bm.reshape(bsz, length, SSD_GROUPS, SSD_STATE)
    cm = cm.reshape(bsz, length, SSD_GROUPS, SSD_STATE)
    dt = jax.nn.softplus(dt.astype(f32) + dt_bias.astype(f32)).reshape(bsz, length, SSD_GROUPS, SSD_HPG)
    da = dt * (-jnp.exp(a_log.astype(f32))).reshape(SSD_GROUPS, SSD_HPG)
    s0 = s0.astype(f32).reshape(bsz, SSD_GROUPS, SSD_HPG, SSD_HEADDIM, SSD_STATE)
    y, s_new = scan_chunks(ssd_chunk_step, s0, (xs * dt[..., None], da, bm, cm), segments)
    y = y + xs * d_skip.astype(f32).reshape(SSD_GROUPS, SSD_HPG, 1)
    y = y.reshape(bsz, length, SSD_INNER) * jax.nn.silu(z.astype(f32))
    y = rms_normalize(y.reshape(bsz, length, SSD_GROUPS, SSD_INNER // SSD_GROUPS)).reshape(bsz, length, SSD_INNER)
    y = y * norm_g.astype(f32)
    return y, new_hist, s_new.reshape(bsz, SSD_HEADS, SSD_HEADDIM, SSD_STATE)


def hier_moe(x, w_group, b_group, w_expert, b_expert, w1, w3, w2):
    shp = x.shape
    xf = x.reshape(-1, D_MODEL)
    n = xf.shape[0]
    g_logit = (xf @ w_group).astype(jnp.float32) + b_group.astype(jnp.float32)
    g_prob = jax.nn.softmax(g_logit, axis=-1)
    g_idx = jnp.argmax(g_logit, axis=-1)
    g_w = jnp.take_along_axis(g_prob, g_idx[:, None], axis=-1)
    e_logit = ((xf @ w_expert).astype(jnp.float32) + b_expert.astype(jnp.float32)).reshape(n, MOE_GROUPS, MOE_EPG)
    e_logit = jnp.take_along_axis(e_logit, g_idx[:, None, None], axis=1)[:, 0]
    e_prob = jax.nn.softmax(e_logit, axis=-1)
    top_w, top_i = lax.top_k(e_prob, MOE_TOPK)
    top_w = top_w / jnp.sum(top_w, axis=-1, keepdims=True)
    gate = g_w * jnp.einsum('nk,nke->ne', top_w, jax.nn.one_hot(top_i, MOE_EPG, dtype=jnp.float32))
    out = jnp.zeros((n, D_MODEL), jnp.float32)
    for g in range(MOE_GROUPS):
        gate_g = jnp.where((g_idx == g)[:, None], gate, 0.0).astype(x.dtype)
        h = jax.nn.silu(jnp.einsum('nd,edf->nef', xf, w1[g])) * jnp.einsum('nd,edf->nef', xf, w3[g])
        out = out + jnp.einsum('nef,ne,efd->nd', h, gate_g, w2[g]).astype(jnp.float32)
    return out.astype(x.dtype).reshape(shp)


def run_trunk(x, st_delta, st_dconv, st_lru, st_lconv, st_ssm, st_sconv, prompt, p):
    length = x.shape[1]
    dn_seg = chunk_segments(length, DN_CHUNK, prompt)
    ssd_seg = chunk_segments(length, SSD_CHUNK, prompt)
    out_delta, out_dconv, out_lru, out_lconv, out_ssm, out_sconv = [], [], [], [], [], []
    for layer in range(DEPTH):
        i = layer // 2
        if layer % 2 == 0:
            proj = jnp.einsum('bld,de->ble', x, p['w_in_ab'][i])
            q, k, v, z, b, a, xb, yb = split_last(proj, AB_SPLIT)
            o_a, c_a, s_a = gated_delta_mixer(q, k, v, z, b, a, st_dconv[i], st_delta[i], p['dn_conv_w'][i],
                                              p['dn_a_log'][i], p['dn_dt_bias'][i], p['dn_norm_g'][i], dn_seg)
            o_b, c_b, h_b = rglru_mixer(xb, yb, st_lconv[i], st_lru[i], p['lru_conv_w'][i], p['lru_conv_b'][i],
                                        p['lru_w_a'][i], p['lru_b_a'][i], p['lru_w_x'][i], p['lru_b_x'][i], p['lru_lam'][i])
            mixed = jnp.concatenate([o_a, o_b], axis=-1).astype(x.dtype)
            mix = jnp.einsum('ble,ed->bld', mixed, p['w_out_ab'][i])
            out_delta.append(s_a)
            out_dconv.append(c_a)
            out_lru.append(h_b)
            out_lconv.append(c_b)
        else:
            proj = jnp.einsum('bld,de->ble', x, p['w_in_ssd'][i])
            o_c, c_c, s_c = ssd_mixer(proj, st_sconv[i], st_ssm[i], p['ssd_conv_w'][i], p['ssd_conv_b'][i],
                                      p['ssd_dt_bias'][i], p['ssd_a_log'][i], p['ssd_d'][i], p['ssd_norm_g'][i], ssd_seg)
            mix = jnp.einsum('ble,ed->bld', o_c.astype(x.dtype), p['w_out_ssd'][i])
            out_ssm.append(s_c)
            out_sconv.append(c_c)
        x = layer_norm(ALPHA * x + mix, p['ln_g'][layer, 0], p['ln_b'][layer, 0])
        ffn = hier_moe(x, p['moe_w_group'][layer], p['moe_b_group'][layer], p['moe_w_expert'][layer],
                       p['moe_b_expert'][layer], p['moe_w1'][layer], p['moe_w3'][layer], p['moe_w2'][layer])
        x = layer_norm(ALPHA * x + ffn, p['ln_g'][layer, 1], p['ln_b'][layer, 1])
    dt = x.dtype
    return (x,
            jnp.stack(out_delta, axis=0).astype(dt), jnp.stack(out_dconv, axis=0).astype(dt),
            jnp.stack(out_lru, axis=0).astype(dt), jnp.stack(out_lconv, axis=0).astype(dt),
            jnp.stack(out_ssm, axis=0).astype(dt), jnp.stack(out_sconv, axis=0).astype(dt))


def setup_inputs(seed: int = 0) -> dict:
    key = jax.random.key(seed)
    ks = iter(jax.random.split(key, 48))
    f32 = jnp.float32

    def nrm(shape, scale):
        return jax.random.normal(next(ks), shape, f32) * scale

    def unif(shape, lo, hi):
        return jax.random.uniform(next(ks), shape, f32, lo, hi)

    def dt_bias(shape):
        dt = jnp.exp(unif(shape, math.log(1e-3), math.log(1e-1)))
        return dt + jnp.log(-jnp.expm1(-dt))

    lru_a0 = unif((N_AB, LRU_WIDTH), 0.9, 0.999) ** (1.0 / LRU_C)
    return {
        'x_prompt': nrm((BATCH, SEQ, D_MODEL), 1.0),
        'x_sample': nrm((DEC_BATCH, DEC_SEQ, D_MODEL), 1.0),
        'state_delta': nrm((N_AB, DEC_BATCH, DN_HEADS, DN_DK, DN_DV), 0.2),
        'state_delta_conv': nrm((N_AB, DEC_BATCH, CONV_W - 1, DN_CONV_CH), 1.0),
        'state_lru': nrm((N_AB, DEC_BATCH, LRU_WIDTH), 0.5),
        'state_lru_conv': nrm((N_AB, DEC_BATCH, CONV_W - 1, LRU_WIDTH), 1.0),
        'state_ssm': nrm((N_SSD, DEC_BATCH, SSD_HEADS, SSD_HEADDIM, SSD_STATE), 0.1),
        'state_ssm_conv': nrm((N_SSD, DEC_BATCH, CONV_W - 1, SSD_CONV_CH), 1.0),
        'meta': nrm((N_META, D_MODEL), 1.0),
        'w_in_ab': nrm((N_AB, D_MODEL, AB_IN), D_MODEL ** -0.5),
        'dn_conv_w': nrm((N_AB, CONV_W, DN_CONV_CH), CONV_W ** -0.5),
        'dn_a_log': jnp.log(unif((N_AB, DN_HEADS), 1.0, 16.0)),
        'dn_dt_bias': dt_bias((N_AB, DN_HEADS)),
        'dn_norm_g': 1.0 + nrm((N_AB, DN_DV), 0.02),
        'lru_conv_w': nrm((N_AB, CONV_W, LRU_WIDTH), CONV_W ** -0.5),
        'lru_conv_b': nrm((N_AB, LRU_WIDTH), 0.02),
        'lru_w_a': nrm((N_AB, LRU_BLOCKS, LRU_BD, LRU_BD), LRU_BD ** -0.5),
        'lru_b_a': nrm((N_AB, LRU_WIDTH), 0.02),
        'lru_w_x': nrm((N_AB, LRU_BLOCKS, LRU_BD, LRU_BD), LRU_BD ** -0.5),
        'lru_b_x': nrm((N_AB, LRU_WIDTH), 0.02),
        'lru_lam': jnp.log(lru_a0) - jnp.log1p(-lru_a0),
        'w_out_ab': nrm((N_AB, AB_MIX, D_MODEL), BETA * AB_MIX ** -0.5),
        'w_in_ssd': nrm((N_SSD, D_MODEL, SSD_IN), D_MODEL ** -0.5),
        'ssd_conv_w': nrm((N_SSD, CONV_W, SSD_CONV_CH), CONV_W ** -0.5),
        'ssd_conv_b': nrm((N_SSD, SSD_CONV_CH), 0.02),
        'ssd_dt_bias': dt_bias((N_SSD, SSD_HEADS)),
        'ssd_a_log': jnp.log(unif((N_SSD, SSD_HEADS), 1.0, 16.0)),
        'ssd_d': 1.0 + nrm((N_SSD, SSD_HEADS), 0.02),
        'ssd_norm_g': 1.0 + nrm((N_SSD, SSD_INNER), 0.02),
        'w_out_ssd': nrm((N_SSD, SSD_INNER, D_MODEL), BETA * SSD_INNER ** -0.5),
        'ln_g': 1.0 + nrm((DEPTH, 2, D_MODEL), 0.02),
        'ln_b': nrm((DEPTH, 2, D_MODEL), 0.02),
        'moe_w_group': nrm((DEPTH, D_MODEL, MOE_GROUPS), D_MODEL ** -0.5),
        'moe_b_group': nrm((DEPTH, MOE_GROUPS), 0.01),
        'moe_w_expert': nrm((DEPTH, D_MODEL, MOE_GROUPS * MOE_EPG), D_MODEL ** -0.5),
        'moe_b_expert': nrm((DEPTH, MOE_GROUPS * MOE_EPG), 0.01),
        'moe_w1': nrm((DEPTH, MOE_GROUPS, MOE_EPG, D_MODEL, MOE_HIDDEN), D_MODEL ** -0.5),
        'moe_w3': nrm((DEPTH, MOE_GROUPS, MOE_EPG, D_MODEL, MOE_HIDDEN), D_MODEL ** -0.5),
        'moe_w2': nrm((DEPTH, MOE_GROUPS, MOE_EPG, MOE_HIDDEN, D_MODEL), BETA * MOE_HIDDEN ** -0.5),
    }


def reference(x_prompt, x_sample, state_delta, state_delta_conv, state_lru, state_lru_conv, state_ssm, state_ssm_conv,
              meta, w_in_ab, dn_conv_w, dn_a_log, dn_dt_bias, dn_norm_g, lru_conv_w, lru_conv_b, lru_w_a, lru_b_a,
              lru_w_x, lru_b_x, lru_lam, w_out_ab, w_in_ssd, ssd_conv_w, ssd_conv_b, ssd_dt_bias, ssd_a_log, ssd_d,
              ssd_norm_g, w_out_ssd, ln_g, ln_b, moe_w_group, moe_b_group, moe_w_expert, moe_b_expert, moe_w1, moe_w3,
              moe_w2):
    p = dict(w_in_ab=w_in_ab, dn_conv_w=dn_conv_w, dn_a_log=dn_a_log, dn_dt_bias=dn_dt_bias, dn_norm_g=dn_norm_g,
             lru_conv_w=lru_conv_w, lru_conv_b=lru_conv_b, lru_w_a=lru_w_a, lru_b_a=lru_b_a, lru_w_x=lru_w_x,
             lru_b_x=lru_b_x, lru_lam=lru_lam, w_out_ab=w_out_ab, w_in_ssd=w_in_ssd, ssd_conv_w=ssd_conv_w,
             ssd_conv_b=ssd_conv_b, ssd_dt_bias=ssd_dt_bias, ssd_a_log=ssd_a_log, ssd_d=ssd_d, ssd_norm_g=ssd_norm_g,
             w_out_ssd=w_out_ssd, ln_g=ln_g, ln_b=ln_b, moe_w_group=moe_w_group, moe_b_group=moe_b_group,
             moe_w_expert=moe_w_expert, moe_b_expert=moe_b_expert, moe_w1=moe_w1, moe_w3=moe_w3, moe_w2=moe_w2)
    bp = x_prompt.shape[0]
    dt = x_prompt.dtype
    xp = jnp.concatenate([jnp.broadcast_to(meta.astype(dt), (bp, N_META, D_MODEL)), x_prompt], axis=1)
    z_delta = jnp.zeros((N_AB, bp, DN_HEADS, DN_DK, DN_DV), dt)
    z_dconv = jnp.zeros((N_AB, bp, CONV_W - 1, DN_CONV_CH), dt)
    z_lru = jnp.zeros((N_AB, bp, LRU_WIDTH), dt)
    z_lconv = jnp.zeros((N_AB, bp, CONV_W - 1, LRU_WIDTH), dt)
    z_ssm = jnp.zeros((N_SSD, bp, SSD_HEADS, SSD_HEADDIM, SSD_STATE), dt)
    z_sconv = jnp.zeros((N_SSD, bp, CONV_W - 1, SSD_CONV_CH), dt)
    yp, p_delta, p_dconv, p_lru, p_lconv, p_ssm, p_sconv = run_trunk(
        xp, z_delta, z_dconv, z_lru, z_lconv, z_ssm, z_sconv, True, p)
    y_prompt = yp[:, N_META:]
    y_sample, s_delta, s_dconv, s_lru, s_lconv, s_ssm, s_sconv = run_trunk(
        x_sample, state_delta, state_delta_conv, state_lru, state_lru_conv, state_ssm, state_ssm_conv, False, p)
    return (y_prompt, y_sample, p_delta, p_dconv, p_lru, p_lconv, p_ssm, p_sconv,
            s_delta, s_dconv, s_lru, s_lconv, s_ssm, s_sconv)
```

```python
import functools
import math

import jax
import jax.numpy as jnp
from jax import lax
from jax.experimental import pallas as pl
from jax.experimental.pallas import tpu as pltpu

F32 = jnp.float32
BF16 = jnp.bfloat16
HI = lax.Precision.HIGHEST

LN_EPS = 1e-5
RMS_EPS = 1e-6
LRU_C = 8.0
N_META = 16
CONV_W = 4
HIST = CONV_W - 1

LANE = 128
EXT0 = 8
DN_CHUNK = 64
SSD_CHUNK = 128
ROW_TILE = 512
MOE_TILE = 128
PAIR_SLOTS = 64
ROUTE_LANES = 128
VMEM_LIMIT = 56 * 1024 * 1024


def _silu(x):
    return x * jax.nn.sigmoid(x)


def _dot(a, b, **kw):
    return jnp.dot(a, b, preferred_element_type=F32, **kw)


def _dot_nt(a, b, **kw):
    return lax.dot_general(a, b, (((1,), (1,)), ((), ())), preferred_element_type=F32, **kw)


def _dot_tn(a, b, **kw):
    return lax.dot_general(a, b, (((0,), (0,)), ((), ())), preferred_element_type=F32, **kw)


def _iota2(shape, dim):
    return lax.broadcasted_iota(jnp.int32, shape, dim)


def _cumsum_rows(x):
    t = x.shape[0]
    tri = (_iota2((t, t), 0) >= _iota2((t, t), 1)).astype(F32)
    cs = _dot(tri, x, precision=HI)
    eye = (_iota2((LANE, LANE), 0) == _iota2((LANE, LANE), 1)).astype(F32)
    cst = _dot_nt(eye, cs, precision=HI)
    return cs, cst


def _causal_conv(ext_ref, x, w, t):
    ext_ref[pl.ds(EXT0, t), :] = x
    y = ext_ref[pl.ds(EXT0 - HIST, t), :] * w[0:1, :]
    for k in range(1, CONV_W):
        y = y + ext_ref[pl.ds(EXT0 - HIST + k, t), :] * w[k:k + 1, :]
    return y


def _inproj_kernel(x_ref, w_ref, o_ref, *, tn):
    xb = x_ref[...].astype(BF16)
    for j in range(o_ref.shape[1] // tn):
        o_ref[:, j * tn:(j + 1) * tn] = _dot(xb, w_ref[:, j * tn:(j + 1) * tn])


def _inproj(x, w, tm):
    n, k = x.shape
    e = w.shape[1]
    tn = max(c for c in range(LANE, 1024 + LANE, LANE) if e % c == 0)
    return pl.pallas_call(
        functools.partial(_inproj_kernel, tn=tn),
        grid=(n // tm,),
        in_specs=[pl.BlockSpec((tm, k), lambda i: (i, 0)),
                  pl.BlockSpec((k, e), lambda i: (0, 0), pipeline_mode=pl.Buffered(1))],
        out_specs=pl.BlockSpec((tm, e), lambda i: (i, 0)),
        out_shape=jax.ShapeDtypeStruct((n, e), F32),
        compiler_params=pltpu.CompilerParams(dimension_semantics=("parallel",), vmem_limit_bytes=VMEM_LIMIT),
        name="inproj")(x, w)


def _ab_kernel(qkv_ref, z_ref, xb_ref, yb_ref, bg_ref, dconv0_ref, s0_ref, lconv0_ref, h0_ref,
               dnw_ref, dnv_ref, lw_ref, lv_ref, wa_ref, wx_ref, *rest,
               t, length, heads):
    mixed_ref, s_out, dconv_out, h_out, lconv_out, qkv_ext, xb_ext, s_scr, h_scr = rest
    c = pl.program_id(1)
    c_last = (length - 1) // t
    v_last = length - c_last * t
    dk = LANE
    nqk = heads * dk
    cdt = BF16 if t % 16 == 0 else F32

    @pl.when(c == 0)
    def _():
        qkv_ext[pl.ds(EXT0 - HIST, HIST), :] = dconv0_ref[0]
        xb_ext[pl.ds(EXT0 - HIST, HIST), :] = lconv0_ref[0]
        s_scr[...] = s0_ref[0]
        h_scr[...] = h0_ref[0]

    rowmask = _iota2((t, 1), 0) < (length - c * t)

    x = jnp.where(rowmask, qkv_ref[...], 0.0)
    qkv = _silu(_causal_conv(qkv_ext, x, dnw_ref[...], t))

    @pl.when(c == c_last)
    def _():
        dconv_out[0] = qkv_ext[pl.ds(EXT0 + v_last - HIST, HIST), :]

    qkv_ext[pl.ds(EXT0 - HIST, HIST), :] = qkv_ext[pl.ds(EXT0 + t - HIST, HIST), :]

    bg = jnp.where(rowmask, bg_ref[...], 0.0)
    a_log = dnv_ref[0:1, :]
    dt_bias = dnv_ref[1:2, :]
    norm_g = dnv_ref[2:3, :]
    beta = jnp.where(rowmask, jax.nn.sigmoid(bg[:, :LANE]), 0.0)
    g = jnp.where(rowmask, -jnp.exp(a_log) * jax.nn.softplus(bg[:, LANE:] + dt_bias), 0.0)
    gc, gct = _cumsum_rows(g)
    eg = jnp.exp(gc)
    e_last = jnp.exp(gc[t - 1:t, :])
    k_dec = jnp.exp(gc[t - 1:t, :] - gc)
    row = _iota2((t, t), 0)
    col = _iota2((t, t), 1)
    nsteps = max(1, (t - 1).bit_length())
    z = z_ref[...]
    for h in range(heads):
        q = qkv[:, h * dk:(h + 1) * dk]
        k = qkv[:, nqk + h * dk:nqk + (h + 1) * dk]
        v = qkv[:, 2 * nqk + h * dk:2 * nqk + (h + 1) * dk]
        q = q * lax.rsqrt(jnp.sum(q * q, axis=-1, keepdims=True) + RMS_EPS) * (dk ** -0.5)
        k = k * lax.rsqrt(jnp.sum(k * k, axis=-1, keepdims=True) + RMS_EPS)
        bh = beta[:, h:h + 1]
        decay = jnp.where(row >= col, jnp.exp(gc[:, h:h + 1] - gct[h:h + 1, :]), 0.0)
        kb = k * bh
        kc = k.astype(cdt)
        a_mat = jnp.where(row > col, _dot_nt(kb.astype(cdt), kc) * decay, 0.0)
        y = jnp.concatenate([v * bh, kb * eg[:, h:h + 1]], axis=-1)
        p = -a_mat
        for i in range(nsteps):
            pc = p.astype(cdt)
            y = y + _dot(pc, y.astype(cdt))
            if i + 1 < nsteps:
                p = _dot(pc, pc)
        u = y[:, :dk]
        w = y[:, dk:]
        s = s_scr[h]
        sc = s.astype(cdt)
        v_new = u - _dot(w.astype(cdt), sc)
        attn = _dot_nt(q.astype(cdt), kc) * decay
        o = _dot((q * eg[:, h:h + 1]).astype(cdt), sc) + _dot(attn.astype(cdt), v_new.astype(cdt))
        s_scr[h] = s * e_last[:, h:h + 1] + _dot_tn((k * k_dec[:, h:h + 1]).astype(cdt), v_new.astype(cdt))
        o = o * lax.rsqrt(jnp.mean(o * o, axis=-1, keepdims=True) + RMS_EPS) * norm_g
        mixed_ref[:, h * dk:(h + 1) * dk] = jnp.where(rowmask, o * _silu(z[:, h * dk:(h + 1) * dk]), 0.0)

    xb = jnp.where(rowmask, xb_ref[...], 0.0)
    xc = _causal_conv(xb_ext, xb, lw_ref[...], t) + lv_ref[0:1, :]

    @pl.when(c == c_last)
    def _():
        lconv_out[0] = xb_ext[pl.ds(EXT0 + v_last - HIST, HIST), :]

    xb_ext[pl.ds(EXT0 - HIST, HIST), :] = xb_ext[pl.ds(EXT0 + t - HIST, HIST), :]
    xcb = xc.astype(BF16)
    r = jax.nn.sigmoid(_dot(xcb, wa_ref[...]) + lv_ref[1:2, :])
    gi = jax.nn.sigmoid(_dot(xcb, wx_ref[...]) + lv_ref[2:3, :])
    log_a = jnp.where(rowmask, -LRU_C * r * jax.nn.softplus(-lv_ref[3:4, :]), 0.0)
    a = jnp.exp(log_a)
    uu = jnp.where(rowmask, jnp.sqrt(1.0 - a * a) * (gi * xc), 0.0)
    rows = _iota2((t, 1), 0)
    sh = 1
    while sh < t:
        keep = rows >= sh
        uu = jnp.where(keep, a * pltpu.roll(uu, sh, 0) + uu, uu)
        a = jnp.where(keep, a * pltpu.roll(a, sh, 0), a)
        sh *= 2
    hh = uu + a * h_scr[...]
    h_scr[...] = hh[t - 1:t, :]
    yb = yb_ref[...]
    gelu = 0.5 * yb * (1.0 + jnp.tanh(math.sqrt(2.0 / math.pi) * (yb + 0.044715 * (yb * yb * yb))))
    mixed_ref[:, nqk:] = jnp.where(rowmask, gelu * hh, 0.0)

    @pl.when(c == c_last)
    def _():
        s_out[0] = s_scr[...]
        h_out[0] = h_scr[...]


def _ab_mixer(proj, dconv0, s0, lconv0, h0, wts, *, t, length, batch, cpb, rb0):
    dnw, dnv, lw, lv, wa, wx = wts
    heads = s0.shape[1]
    nqk = heads * LANE
    nv = heads * s0.shape[3]
    lw_ch = lw.shape[1]
    dmix = nv + lw_ch
    rmap = lambda j: (lambda b, c: (rb0 + b * cpb + c, j))
    bmap3 = lambda b, c: (b, 0, 0)
    bmap4 = lambda b, c: (b, 0, 0, 0)
    full2 = lambda b, c: (0, 0)
    in_specs = [
        pl.BlockSpec((t, 3 * nqk), rmap(0)),
        pl.BlockSpec((t, nv), rmap(3 * nqk // nv)),
        pl.BlockSpec((t, lw_ch), rmap((3 * nqk + nv) // lw_ch)),
        pl.BlockSpec((t, lw_ch), rmap((3 * nqk + nv) // lw_ch + 1)),
        pl.BlockSpec((t, 2 * LANE), rmap((3 * nqk + nv + 2 * lw_ch) // (2 * LANE))),
        pl.BlockSpec((1, HIST, 3 * nqk), bmap3),
        pl.BlockSpec((1,) + s0.shape[1:], bmap4),
        pl.BlockSpec((1, HIST, lw_ch), bmap3),
        pl.BlockSpec((1, 1, lw_ch), bmap3),
        pl.BlockSpec(dnw.shape, full2), pl.BlockSpec(dnv.shape, full2),
        pl.BlockSpec(lw.shape, full2), pl.BlockSpec(lv.shape, full2),
        pl.BlockSpec(wa.shape, full2), pl.BlockSpec(wx.shape, full2),
    ]
    args = [proj, proj, proj, proj, proj, dconv0, s0, lconv0, h0, dnw, dnv, lw, lv, wa, wx]
    out_shape = (jax.ShapeDtypeStruct((batch * cpb * t, dmix), F32),
                 jax.ShapeDtypeStruct(s0.shape, F32),
                 jax.ShapeDtypeStruct(dconv0.shape, F32),
                 jax.ShapeDtypeStruct(h0.shape, F32),
                 jax.ShapeDtypeStruct(lconv0.shape, F32))
    out_specs = (pl.BlockSpec((t, dmix), lambda b, c: (b * cpb + c, 0)),
                 pl.BlockSpec((1,) + s0.shape[1:], bmap4),
                 pl.BlockSpec((1, HIST, 3 * nqk), bmap3),
                 pl.BlockSpec((1, 1, lw_ch), bmap3),
                 pl.BlockSpec((1, HIST, lw_ch), bmap3))
    return pl.pallas_call(
        functools.partial(_ab_kernel, t=t, length=length, heads=heads),
        grid=(batch, cpb),
        in_specs=in_specs, out_specs=out_specs, out_shape=out_shape,
        scratch_shapes=[pltpu.VMEM((t + EXT0, 3 * nqk), F32), pltpu.VMEM((t + EXT0, lw_ch), F32),
                        pltpu.VMEM(s0.shape[1:], F32), pltpu.VMEM((1, lw_ch), F32)],
        compiler_params=pltpu.CompilerParams(dimension_semantics=("parallel", "arbitrary"),
                                             vmem_limit_bytes=VMEM_LIMIT),
        name="ab_mixer")(*args)


def _ssd_kernel(z_ref, xs_ref, bc_ref, dt_ref, xconv0_ref, bconv0_ref, s0_ref,
                wxs_ref, wbc_ref, bxs_ref, bbc_ref, vec_ref, ng_ref, *rest,
                t, length, heads, groups):
    mixed_ref, s_out, conv_out, xs_ext, bc_ext, s_scr, ybuf = rest
    c = pl.program_id(1)
    c_last = (length - 1) // t
    v_last = length - c_last * t
    inner = xs_ref.shape[1]
    p = inner // heads
    n = s_scr.shape[2]
    hpg = heads // groups
    gw = inner // groups
    cdt = BF16 if t % 16 == 0 else F32

    @pl.when(c == 0)
    def _():
        xs_ext[pl.ds(EXT0 - HIST, HIST), :] = xconv0_ref[0]
        bc_ext[pl.ds(EXT0 - HIST, HIST), :] = bconv0_ref[0]
        s_scr[...] = s0_ref[0]

    rowmask = _iota2((t, 1), 0) < (length - c * t)
    xs = _silu(_causal_conv(xs_ext, jnp.where(rowmask, xs_ref[...], 0.0), wxs_ref[...], t) + bxs_ref[...])
    bc = _silu(_causal_conv(bc_ext, jnp.where(rowmask, bc_ref[...], 0.0), wbc_ref[...], t) + bbc_ref[...])

    @pl.when(c == c_last)
    def _():
        conv_out[0, :, :inner] = xs_ext[pl.ds(EXT0 + v_last - HIST, HIST), :]
        conv_out[0, :, inner:] = bc_ext[pl.ds(EXT0 + v_last - HIST, HIST), :]

    xs_ext[pl.ds(EXT0 - HIST, HIST), :] = xs_ext[pl.ds(EXT0 + t - HIST, HIST), :]
    bc_ext[pl.ds(EXT0 - HIST, HIST), :] = bc_ext[pl.ds(EXT0 + t - HIST, HIST), :]

    dt = jnp.where(rowmask, jax.nn.softplus(jnp.where(rowmask, dt_ref[...], 0.0) + vec_ref[0:1, :]), 0.0)
    da = dt * (-jnp.exp(vec_ref[1:2, :]))
    d_skip = vec_ref[2:3, :]
    cs, cst = _cumsum_rows(da)
    ecs = jnp.exp(cs)
    e_last = jnp.exp(cs[t - 1:t, :])
    w_dec = jnp.exp(cs[t - 1:t, :] - cs)
    causal = _iota2((t, t), 0) >= _iota2((t, t), 1)
    gn = groups * n
    for g in range(groups):
        bm = bc[:, g * n:(g + 1) * n].astype(cdt)
        cm = bc[:, gn + g * n:gn + (g + 1) * n].astype(cdt)
        cb = _dot_nt(cm, bm)
        for j in range(hpg):
            e = g * hpg + j
            decay = jnp.where(causal, jnp.exp(cs[:, e:e + 1] - cst[e:e + 1, :]), 0.0)
            xe = xs[:, e * p:(e + 1) * p]
            xdt = xe * dt[:, e:e + 1]
            s = s_scr[e]
            y = _dot((cb * decay).astype(cdt), xdt.astype(cdt)) + _dot_nt(cm, s.astype(cdt)) * ecs[:, e:e + 1]
            s_scr[e] = s * e_last[:, e:e + 1] + _dot_tn((xdt * w_dec[:, e:e + 1]).astype(cdt), bm)
            ybuf[:, e * p:(e + 1) * p] = y + xe * d_skip[:, e:e + 1]
    yz = ybuf[...] * _silu(z_ref[...])
    ng = ng_ref[...]
    for g in range(groups):
        seg = yz[:, g * gw:(g + 1) * gw]
        seg = seg * lax.rsqrt(jnp.mean(seg * seg, axis=-1, keepdims=True) + RMS_EPS) * ng[:, g * gw:(g + 1) * gw]
        mixed_ref[:, g * gw:(g + 1) * gw] = jnp.where(rowmask, seg, 0.0)

    @pl.when(c == c_last)
    def _():
        s_out[0] = s_scr[...]


def _ssd_mixer(proj, conv0, s0, wts, *, t, length, batch, cpb, rb0, groups):
    cw, cb, vec, ng = wts
    heads = s0.shape[1]
    inner = heads * s0.shape[2]
    rmap = lambda j: (lambda b, c: (rb0 + b * cpb + c, j))
    bmap4 = lambda b, c: (b, 0, 0, 0)
    in_specs = [
        pl.BlockSpec((t, inner), rmap(0)), pl.BlockSpec((t, inner), rmap(1)), pl.BlockSpec((t, inner), rmap(2)),
        pl.BlockSpec((t, LANE), rmap(3 * inner // LANE)),
        pl.BlockSpec((1, HIST, inner), lambda b, c: (b, 0, 0)),
        pl.BlockSpec((1, HIST, inner), lambda b, c: (b, 0, 1)),
        pl.BlockSpec((1,) + s0.shape[1:], bmap4),
        pl.BlockSpec((CONV_W, inner), lambda b, c: (0, 0)), pl.BlockSpec((CONV_W, inner), lambda b, c: (0, 1)),
        pl.BlockSpec((1, inner), lambda b, c: (0, 0)), pl.BlockSpec((1, inner), lambda b, c: (0, 1)),
        pl.BlockSpec(vec.shape, lambda b, c: (0, 0)), pl.BlockSpec(ng.shape, lambda b, c: (0, 0)),
    ]
    args = [proj, proj, proj, proj, conv0, conv0, s0, cw, cw, cb, cb, vec, ng]
    out_shape = (jax.ShapeDtypeStruct((batch * cpb * t, inner), F32),
                 jax.ShapeDtypeStruct(s0.shape, F32),
                 jax.ShapeDtypeStruct(conv0.shape, F32))
    out_specs = (pl.BlockSpec((t, inner), lambda b, c: (b * cpb + c, 0)),
                 pl.BlockSpec((1,) + s0.shape[1:], bmap4),
                 pl.BlockSpec((1, HIST, conv0.shape[2]), lambda b, c: (b, 0, 0)))
    return pl.pallas_call(
        functools.partial(_ssd_kernel, t=t, length=length, heads=heads, groups=groups),
        grid=(batch, cpb),
        in_specs=in_specs, out_specs=out_specs, out_shape=out_shape,
        scratch_shapes=[pltpu.VMEM((t + EXT0, inner), F32), pltpu.VMEM((t + EXT0, inner), F32),
                        pltpu.VMEM(s0.shape[1:], F32), pltpu.VMEM((t, inner), F32)],
        compiler_params=pltpu.CompilerParams(dimension_semantics=("parallel", "arbitrary"),
                                             vmem_limit_bytes=VMEM_LIMIT),
        name="ssd_mixer")(*args)


def _layer_norm(h, g, b):
    mu = jnp.mean(h, axis=-1, keepdims=True)
    d = h - mu
    var = jnp.mean(d * d, axis=-1, keepdims=True)
    return d * lax.rsqrt(var + LN_EPS) * g + b


def _outln_kernel(yp_ref, ys_ref, x_ref, w_ref, ln_ref, wr_ref, br_ref, o_ref, ybuf, *, alpha, n_groups, epg, nbp):
    d = x_ref.shape[1]
    i = pl.program_id(0)

    @pl.when(i < nbp)
    def _():
        ybuf[...] = yp_ref[...].astype(BF16)

    @pl.when(i >= nbp)
    def _():
        ybuf[...] = ys_ref[...].astype(BF16)

    acc = _dot(ybuf[...], w_ref[...])
    xn = _layer_norm(alpha * x_ref[...] + acc, ln_ref[0:1, :], ln_ref[1:2, :])
    o_ref[:, :d] = xn
    logits = _dot(xn, wr_ref[...], precision=HI) + br_ref[...]
    tm = logits.shape[0]
    lane = _iota2((tm, ROUTE_LANES), 1)
    ninf = -jnp.inf
    gl = jnp.where(lane < n_groups, logits, ninf)
    gmax = jnp.max(gl, axis=-1, keepdims=True)
    gidx = jnp.min(jnp.where(gl == gmax, lane, ROUTE_LANES), axis=-1, keepdims=True)
    g_w = 1.0 / jnp.sum(jnp.where(lane < n_groups, jnp.exp(logits - gmax), 0.0), axis=-1, keepdims=True)
    lo = n_groups + epg * gidx
    el = jnp.where((lane >= lo) & (lane < lo + epg), logits, ninf)
    m1 = jnp.max(el, axis=-1, keepdims=True)
    i1 = jnp.min(jnp.where(el == m1, lane, ROUTE_LANES), axis=-1, keepdims=True)
    el2 = jnp.where(lane == i1, ninf, el)
    m2 = jnp.max(el2, axis=-1, keepdims=True)
    i2 = jnp.min(jnp.where(el2 == m2, lane, ROUTE_LANES), axis=-1, keepdims=True)
    r = jnp.exp(m2 - m1)
    w1 = g_w / (1.0 + r)
    w2 = g_w * r / (1.0 + r)
    e1 = i1 - lo
    e2 = i2 - lo
    first = e1 < e2
    ea = jnp.where(first, e1, e2)
    eb = jnp.where(first, e2, e1)
    bucket = (gidx * PAIR_SLOTS + ea * epg + eb).astype(F32)
    ga = jnp.where(first, w1, w2)
    gb = jnp.where(first, w2, w1)
    o_ref[:, d:] = jnp.where(lane == 0, bucket, jnp.where(lane == 1, ga, jnp.where(lane == 2, gb, 0.0)))


def _outln(mixed_p, mixed_s, x, w, ln, wr, br, *, tm, alpha, n_groups, epg):
    n, d = x.shape
    dm = mixed_p.shape[1]
    nbp = mixed_p.shape[0] // tm
    return pl.pallas_call(
        functools.partial(_outln_kernel, alpha=alpha, n_groups=n_groups, epg=epg, nbp=nbp),
        grid=(n // tm,),
        in_specs=[pl.BlockSpec((tm, dm), lambda i: (jnp.minimum(i, nbp - 1), 0)),
                  pl.BlockSpec((tm, dm), lambda i: (jnp.maximum(i - nbp, 0), 0)),
                  pl.BlockSpec((tm, d), lambda i: (i, 0)),
                  pl.BlockSpec((dm, d), lambda i: (0, 0), pipeline_mode=pl.Buffered(1)),
                  pl.BlockSpec(ln.shape, lambda i: (0, 0)),
                  pl.BlockSpec(wr.shape, lambda i: (0, 0)),
                  pl.BlockSpec(br.shape, lambda i: (0, 0))],
        out_specs=pl.BlockSpec((tm, d + ROUTE_LANES), lambda i: (i, 0)),
        out_shape=jax.ShapeDtypeStruct((n, d + ROUTE_LANES), F32),
        scratch_shapes=[pltpu.VMEM((tm, dm), BF16)],
        compiler_params=pltpu.CompilerParams(dimension_semantics=("parallel",), vmem_limit_bytes=VMEM_LIMIT),
        name="outproj_ln_route")(mixed_p, mixed_s, x, w, ln, wr, br)


def _moe_kernel(valid_ref, ea_ref, eb_ref, src_ref, x_hbm, w1a_ref, w3a_ref, w2a_ref, w1b_ref, w3b_ref, w2b_ref,
                ln_ref, o_hbm, xbuf, obuf, sem, *, alpha, tr):
    t = pl.program_id(0)
    d = obuf.shape[1]

    def row_in(r):
        return pltpu.make_async_copy(x_hbm.at[pl.ds(jnp.maximum(src_ref[0, 0, r], 0), 1), :],
                                     xbuf.at[pl.ds(r, 1), :], sem.at[0])

    def row_out(r):
        return pltpu.make_async_copy(obuf.at[pl.ds(r, 1), :],
                                     o_hbm.at[pl.ds(jnp.maximum(src_ref[0, 0, r], 0), 1), :], sem.at[1])

    @pl.when(valid_ref[t] > 0)
    def _():
        def start_in(r, carry):
            row_in(r).start()
            return carry

        def wait_in(r, carry):
            row_in(r).wait()
            return carry

        lax.fori_loop(0, tr, start_in, 0)
        lax.fori_loop(0, tr, wait_in, 0)

        x = xbuf[:, :d]
        ga = xbuf[:, d + 1:d + 2]
        gb = xbuf[:, d + 2:d + 3]
        xb = x.astype(BF16)
        ha = _silu(_dot(xb, w1a_ref[0])) * _dot(xb, w3a_ref[0])
        hb = _silu(_dot(xb, w1b_ref[0])) * _dot(xb, w3b_ref[0])
        ffn = _dot((ha * ga).astype(BF16), w2a_ref[0]) + _dot((hb * gb).astype(BF16), w2b_ref[0])
        obuf[...] = _layer_norm(alpha * x + ffn, ln_ref[0:1, :], ln_ref[1:2, :])

        def start_out(r, carry):
            @pl.when(src_ref[0, 0, r] >= 0)
            def _():
                row_out(r).start()
            return carry

        def wait_out(r, carry):
            @pl.when(src_ref[0, 0, r] >= 0)
            def _():
                row_out(r).wait()
            return carry

        lax.fori_loop(0, tr, start_out, 0)
        lax.fori_loop(0, tr, wait_out, 0)


def _moe(x1r, src, valid, ea, eb, w1, w3, w2, ln, *, alpha, tr):
    n, dr = x1r.shape
    d = dr - ROUTE_LANES
    nt = valid.shape[0]
    f = w1.shape[2]
    wa = lambda t, v, a, b: (a[t], 0, 0)
    wb = lambda t, v, a, b: (b[t], 0, 0)
    grid_spec = pltpu.PrefetchScalarGridSpec(
        num_scalar_prefetch=3,
        grid=(nt,),
        in_specs=[pl.BlockSpec((1, 1, tr), lambda t, v, a, b: (t, 0, 0), memory_space=pltpu.SMEM),
                  pl.BlockSpec(memory_space=pl.ANY),
                  pl.BlockSpec((1, d, f), wa), pl.BlockSpec((1, d, f), wa), pl.BlockSpec((1, f, d), wa),
                  pl.BlockSpec((1, d, f), wb), pl.BlockSpec((1, d, f), wb), pl.BlockSpec((1, f, d), wb),
                  pl.BlockSpec(ln.shape, lambda t, v, a, b: (0, 0))],
        out_specs=pl.BlockSpec(memory_space=pl.ANY),
        scratch_shapes=[pltpu.VMEM((tr, dr), F32), pltpu.VMEM((tr, d), F32), pltpu.SemaphoreType.DMA((2,))])
    return pl.pallas_call(
        functools.partial(_moe_kernel, alpha=alpha, tr=tr),
        grid_spec=grid_spec,
        out_shape=jax.ShapeDtypeStruct((n, d), F32),
        compiler_params=pltpu.CompilerParams(dimension_semantics=("arbitrary",), vmem_limit_bytes=VMEM_LIMIT),
        name="moe_routed")(valid, ea, eb, src, x1r, w1, w3, w2, w1, w3, w2, ln)


def _route_schedule(bucket, *, tr, nt, n_buckets, epg, e_off):
    n = bucket.shape[0]
    order = jnp.argsort(bucket).astype(jnp.int32)
    sb = bucket[order]
    counts = jnp.zeros((n_buckets,), jnp.int32).at[bucket].add(1)
    tiles_b = (counts + tr - 1) // tr
    tile_end = jnp.cumsum(tiles_b)
    pstart = (tile_end - tiles_b) * tr
    ustart = jnp.cumsum(counts) - counts
    dest = pstart[sb] + (jnp.arange(n, dtype=jnp.int32) - ustart[sb])
    src = jnp.full((nt * tr,), -1, jnp.int32).at[dest].set(order)
    tiles = jnp.arange(nt, dtype=jnp.int32)
    valid = tiles < tile_end[-1]
    tb = jnp.searchsorted(tile_end, jnp.minimum(tiles, tile_end[-1] - 1), side="right").astype(jnp.int32)
    grp = tb // PAIR_SLOTS
    pair = tb % PAIR_SLOTS
    ea = e_off + grp * epg + pair // epg
    eb = e_off + grp * epg + pair % epg
    return src.reshape(nt, 1, tr), valid.astype(jnp.int32), ea.astype(jnp.int32), eb.astype(jnp.int32)


def _pad_cols(w, width):
    return jnp.pad(w, ((0, 0), (0, width - w.shape[1])))


def _pad_vec(v, width=LANE):
    return jnp.pad(v.astype(F32), (0, width - v.shape[0]))[None, :]


def _rows8(*vecs):
    m = jnp.stack([v.astype(F32) for v in vecs], axis=0)
    return jnp.pad(m, ((0, 8 - m.shape[0]), (0, 0)))


def _block_diag(w):
    nb, bd, _ = w.shape
    eye = jnp.eye(nb, dtype=w.dtype)
    return jnp.einsum("nde,nm->ndme", w, eye).reshape(nb * bd, nb * bd)


def kernel(x_prompt, x_sample, state_delta, state_delta_conv, state_lru, state_lru_conv, state_ssm, state_ssm_conv,
           meta, w_in_ab, dn_conv_w, dn_a_log, dn_dt_bias, dn_norm_g, lru_conv_w, lru_conv_b, lru_w_a, lru_b_a,
           lru_w_x, lru_b_x, lru_lam, w_out_ab, w_in_ssd, ssd_conv_w, ssd_conv_b, ssd_dt_bias, ssd_a_log, ssd_d,
           ssd_norm_g, w_out_ssd, ln_g, ln_b, moe_w_group, moe_b_group, moe_w_expert, moe_b_expert, moe_w1, moe_w3,
           moe_w2):
    bp, seq, d = x_prompt.shape
    bs, ls, _ = x_sample.shape
    depth = ln_g.shape[0]
    alpha = (2.0 * depth) ** 0.25
    n_groups, epg = moe_w1.shape[1], moe_w1.shape[2]
    heads_dn = dn_a_log.shape[1]
    nqk = heads_dn * state_delta.shape[3]
    nv = heads_dn * state_delta.shape[4]
    lru_w = lru_lam.shape[1]
    heads_ssd = ssd_a_log.shape[1]
    inner = heads_ssd * state_ssm.shape[3]
    n_state = state_ssm.shape[4]
    ssd_groups = (ssd_conv_w.shape[2] - inner) // (2 * n_state)
    assert nqk == nv and state_delta.shape[3] == LANE and state_delta.shape[4] == LANE and n_state == LANE

    lp_len = N_META + seq
    chunk_lcm = math.lcm(DN_CHUNK, SSD_CHUNK)
    lpad = -(-lp_len // chunk_lcm) * chunk_lcm
    np_rows = bp * lpad
    ns_rows = bs * ls
    ntot = np_rows + ns_rows
    tm = math.gcd(math.gcd(np_rows, ns_rows), ROW_TILE)
    assert tm % 8 == 0 and np_rows % ls == 0 and ls % 8 == 0 and ls >= HIST
    assert (lp_len - 1) % DN_CHUNK + 1 >= HIST and (lp_len - 1) % SSD_CHUNK + 1 >= HIST

    xp = jnp.concatenate([jnp.broadcast_to(meta.astype(F32), (bp, N_META, d)), x_prompt,
                          jnp.zeros((bp, lpad - lp_len, d), F32)], axis=1)
    x = jnp.concatenate([xp.reshape(np_rows, d), x_sample.reshape(ns_rows, d)], axis=0)

    tr = MOE_TILE
    n_buckets = n_groups * PAIR_SLOTS
    n_pairs = n_groups * (epg * (epg - 1) // 2)
    nt = -(-(ntot + n_pairs * (tr - 1)) // tr)
    n_exp = n_groups * epg
    w1 = moe_w1.reshape(depth * n_exp, d, -1).astype(BF16)
    w3 = moe_w3.reshape(depth * n_exp, d, -1).astype(BF16)
    w2 = moe_w2.reshape(depth * n_exp, -1, d).astype(BF16)

    zeros = lambda *s: jnp.zeros(s, F32)
    outs = {k: [] for k in ("pd", "pdc", "pl", "plc", "ps", "psc", "sd", "sdc", "sl", "slc", "ss", "ssc")}
    for layer in range(depth):
        i = layer // 2
        if layer % 2 == 0:
            wi = w_in_ab[i]
            o_b = 3 * nqk + nv
            w_in = jnp.concatenate([wi[:, :o_b], wi[:, o_b + 2 * heads_dn:],
                                    _pad_cols(wi[:, o_b:o_b + heads_dn], LANE),
                                    _pad_cols(wi[:, o_b + heads_dn:o_b + 2 * heads_dn], LANE)], axis=1).astype(BF16)
            proj = _inproj(x, w_in, tm)
            wts = (dn_conv_w[i], _rows8(_pad_vec(dn_a_log[i])[0], _pad_vec(dn_dt_bias[i])[0], dn_norm_g[i]),
                   lru_conv_w[i], _rows8(lru_conv_b[i], lru_b_a[i], lru_b_x[i], lru_lam[i]),
                   _block_diag(lru_w_a[i]).astype(BF16), _block_diag(lru_w_x[i]).astype(BF16))
            mixed_p, s_p, dc_p, h_p, lc_p = _ab_mixer(
                proj, zeros(bp, HIST, 3 * nqk), zeros(bp, *state_delta.shape[2:]), zeros(bp, HIST, lru_w),
                zeros(bp, 1, lru_w), wts, t=DN_CHUNK, length=lp_len, batch=bp, cpb=lpad // DN_CHUNK, rb0=0)
            mixed_s, s_s, dc_s, h_s, lc_s = _ab_mixer(
                proj, state_delta_conv[i], state_delta[i], state_lru_conv[i], state_lru[i][:, None, :], wts,
                t=ls, length=ls, batch=bs, cpb=1, rb0=np_rows // ls)
            outs["pd"].append(s_p); outs["pdc"].append(dc_p); outs["pl"].append(h_p[:, 0]); outs["plc"].append(lc_p)
            outs["sd"].append(s_s); outs["sdc"].append(dc_s); outs["sl"].append(h_s[:, 0]); outs["slc"].append(lc_s)
            w_out = w_out_ab[i].astype(BF16)
        else:
            wi = w_in_ssd[i]
            e_raw = wi.shape[1]
            w_in = _pad_cols(wi, -(-e_raw // LANE) * LANE).astype(BF16)
            proj = _inproj(x, w_in, tm)
            wts = (ssd_conv_w[i], ssd_conv_b[i][None, :],
                   _rows8(_pad_vec(ssd_dt_bias[i])[0], _pad_vec(ssd_a_log[i])[0], _pad_vec(ssd_d[i])[0]),
                   ssd_norm_g[i][None, :])
            mixed_p, s_p, c_p = _ssd_mixer(
                proj, zeros(bp, HIST, ssd_conv_w.shape[2]), zeros(bp, *state_ssm.shape[2:]), wts,
                t=SSD_CHUNK, length=lp_len, batch=bp, cpb=lpad // SSD_CHUNK, rb0=0, groups=ssd_groups)
            mixed_s, s_s, c_s = _ssd_mixer(
                proj, state_ssm_conv[i], state_ssm[i], wts,
                t=ls, length=ls, batch=bs, cpb=1, rb0=np_rows // ls, groups=ssd_groups)
            outs["ps"].append(s_p); outs["psc"].append(c_p)
            outs["ss"].append(s_s); outs["ssc"].append(c_s)
            w_out = w_out_ssd[i].astype(BF16)
        wr = _pad_cols(jnp.concatenate([moe_w_group[layer], moe_w_expert[layer]], axis=1), ROUTE_LANES)
        br = _pad_vec(jnp.concatenate([moe_b_group[layer], moe_b_expert[layer]]), ROUTE_LANES)
        x1r = _outln(mixed_p, mixed_s, x, w_out, _rows8(ln_g[layer, 0], ln_b[layer, 0]), wr, br,
                     tm=tm, alpha=alpha, n_groups=n_groups, epg=epg)
        bucket = x1r[:, d].astype(jnp.int32)
        src, valid, ea, eb = _route_schedule(bucket, tr=tr, nt=nt, n_buckets=n_buckets, epg=epg,
                                             e_off=layer * n_exp)
        x = _moe(x1r, src, valid, ea, eb, w1, w3, w2, _rows8(ln_g[layer, 1], ln_b[layer, 1]), alpha=alpha, tr=tr)

    y_prompt = x[:np_rows].reshape(bp, lpad, d)[:, N_META:lp_len]
    y_sample = x[np_rows:].reshape(bs, ls, d)
    st = lambda k: jnp.stack(outs[k], axis=0)
    return (y_prompt, y_sample, st("pd"), st("pdc"), st("pl"), st("plc"), st("ps"), st("psc"),
            st("sd"), st("sdc"), st("sl"), st("slc"), st("ss"), st("ssc"))
```

```python
import functools
import math

import jax
import jax.numpy as jnp
from jax import lax
from jax.experimental import pallas as pl
from jax.experimental.pallas import tpu as pltpu

F32 = jnp.float32
BF16 = jnp.bfloat16
HI = lax.Precision.HIGHEST

LN_EPS = 1e-5
RMS_EPS = 1e-6
LRU_C = 8.0
N_META = 16
CONV_W = 4
HIST = CONV_W - 1

LANE = 128
EXT0 = 8
DN_CHUNK = 64
SSD_CHUNK = 128
ROW_TILE = 512
MOE_TILE = 128
PAIR_SLOTS = 64
ROUTE_LANES = 128
VMEM_LIMIT = 56 * 1024 * 1024


def _silu(x):
    return x * jax.nn.sigmoid(x)


def _dot(a, b, **kw):
    return jnp.dot(a, b, preferred_element_type=F32, **kw)


def _dot_nt(a, b, **kw):
    return lax.dot_general(a, b, (((1,), (1,)), ((), ())), preferred_element_type=F32, **kw)


def _dot_tn(a, b, **kw):
    return lax.dot_general(a, b, (((0,), (0,)), ((), ())), preferred_element_type=F32, **kw)


def _iota2(shape, dim):
    return lax.broadcasted_iota(jnp.int32, shape, dim)


def _cumsum_rows(x):
    t = x.shape[0]
    tri = (_iota2((t, t), 0) >= _iota2((t, t), 1)).astype(F32)
    cs = _dot(tri, x, precision=HI)
    eye = (_iota2((LANE, LANE), 0) == _iota2((LANE, LANE), 1)).astype(F32)
    cst = _dot_nt(eye, cs, precision=HI)
    return cs, cst


def _causal_conv(ext_ref, x, w, t):
    ext_ref[pl.ds(EXT0, t), :] = x
    y = ext_ref[pl.ds(EXT0 - HIST, t), :] * w[0:1, :]
    for k in range(1, CONV_W):
        y = y + ext_ref[pl.ds(EXT0 - HIST + k, t), :] * w[k:k + 1, :]
    return y


def _inproj_kernel(x_ref, w_ref, o_ref, *, tn):
    xb = x_ref[...].astype(BF16)
    for j in range(o_ref.shape[1] // tn):
        o_ref[:, j * tn:(j + 1) * tn] = _dot(xb, w_ref[:, j * tn:(j + 1) * tn])


def _inproj(x, w, tm, n):
    k = x.shape[1]
    e = w.shape[1]
    tn = max(c for c in range(LANE, 1024 + LANE, LANE) if e % c == 0)
    return pl.pallas_call(
        functools.partial(_inproj_kernel, tn=tn),
        grid=(n // tm,),
        in_specs=[pl.BlockSpec((tm, k), lambda i: (i, 0)),
                  pl.BlockSpec((k, e), lambda i: (0, 0), pipeline_mode=pl.Buffered(1))],
        out_specs=pl.BlockSpec((tm, e), lambda i: (i, 0)),
        out_shape=jax.ShapeDtypeStruct((n, e), F32),
        compiler_params=pltpu.CompilerParams(dimension_semantics=("parallel",), vmem_limit_bytes=VMEM_LIMIT),
        name="inproj")(x, w)


def _ab_kernel(qkv_ref, z_ref, xb_ref, yb_ref, bg_ref, dconv0_ref, s0_ref, lconv0_ref, h0_ref,
               dnw_ref, dnv_ref, lw_ref, lv_ref, wa_ref, wx_ref, *rest,
               t, length, heads):
    mixed_ref, s_out, dconv_out, h_out, lconv_out, qkv_ext, xb_ext, s_scr, h_scr = rest
    c = pl.program_id(1)
    c_last = (length - 1) // t
    v_last = length - c_last * t
    dk = LANE
    nqk = heads * dk
    cdt = BF16 if t % 16 == 0 else F32

    @pl.when(c == 0)
    def _():
        qkv_ext[pl.ds(EXT0 - HIST, HIST), :] = dconv0_ref[0]
        xb_ext[pl.ds(EXT0 - HIST, HIST), :] = lconv0_ref[0]
        s_scr[...] = s0_ref[0]
        h_scr[...] = h0_ref[0]

    rowmask = _iota2((t, 1), 0) < (length - c * t)

    x = jnp.where(rowmask, qkv_ref[...], 0.0)
    qkv = _silu(_causal_conv(qkv_ext, x, dnw_ref[...], t))

    @pl.when(c == c_last)
    def _():
        dconv_out[0] = qkv_ext[pl.ds(EXT0 + v_last - HIST, HIST), :]

    qkv_ext[pl.ds(EXT0 - HIST, HIST), :] = qkv_ext[pl.ds(EXT0 + t - HIST, HIST), :]

    bg = jnp.where(rowmask, bg_ref[...], 0.0)
    a_log = dnv_ref[0:1, :]
    dt_bias = dnv_ref[1:2, :]
    norm_g = dnv_ref[2:3, :]
    beta = jnp.where(rowmask, jax.nn.sigmoid(bg[:, :LANE]), 0.0)
    g = jnp.where(rowmask, -jnp.exp(a_log) * jax.nn.softplus(bg[:, LANE:] + dt_bias), 0.0)
    gc, gct = _cumsum_rows(g)
    eg = jnp.exp(gc)
    e_last = jnp.exp(gc[t - 1:t, :])
    k_dec = jnp.exp(gc[t - 1:t, :] - gc)
    row = _iota2((t, t), 0)
    col = _iota2((t, t), 1)
    nsteps = max(1, (t - 1).bit_length())
    z = z_ref[...]
    for h in range(heads):
        q = qkv[:, h * dk:(h + 1) * dk]
        k = qkv[:, nqk + h * dk:nqk + (h + 1) * dk]
        v = qkv[:, 2 * nqk + h * dk:2 * nqk + (h + 1) * dk]
        q = q * lax.rsqrt(jnp.sum(q * q, axis=-1, keepdims=True) + RMS_EPS) * (dk ** -0.5)
        k = k * lax.rsqrt(jnp.sum(k * k, axis=-1, keepdims=True) + RMS_EPS)
        bh = beta[:, h:h + 1]
        decay = jnp.where(row >= col, jnp.exp(gc[:, h:h + 1] - gct[h:h + 1, :]), 0.0)
        kb = k * bh
        kc = k.astype(cdt)
        a_mat = jnp.where(row > col, _dot_nt(kb.astype(cdt), kc) * decay, 0.0)
        y = jnp.concatenate([v * bh, kb * eg[:, h:h + 1]], axis=-1)
        p = -a_mat
        for i in range(nsteps):
            pc = p.astype(cdt)
            y = y + _dot(pc, y.astype(cdt))
            if i + 1 < nsteps:
                p = _dot(pc, pc)
        u = y[:, :dk]
        w = y[:, dk:]
        s = s_scr[h]
        sc = s.astype(cdt)
        v_new = u - _dot(w.astype(cdt), sc)
        attn = _dot_nt(q.astype(cdt), kc) * decay
        o = _dot((q * eg[:, h:h + 1]).astype(cdt), sc) + _dot(attn.astype(cdt), v_new.astype(cdt))
        s_scr[h] = s * e_last[:, h:h + 1] + _dot_tn((k * k_dec[:, h:h + 1]).astype(cdt), v_new.astype(cdt))
        o = o * lax.rsqrt(jnp.mean(o * o, axis=-1, keepdims=True) + RMS_EPS) * norm_g
        mixed_ref[:, h * dk:(h + 1) * dk] = jnp.where(rowmask, o * _silu(z[:, h * dk:(h + 1) * dk]), 0.0)

    xb = jnp.where(rowmask, xb_ref[...], 0.0)
    xc = _causal_conv(xb_ext, xb, lw_ref[...], t) + lv_ref[0:1, :]

    @pl.when(c == c_last)
    def _():
        lconv_out[0] = xb_ext[pl.ds(EXT0 + v_last - HIST, HIST), :]

    xb_ext[pl.ds(EXT0 - HIST, HIST), :] = xb_ext[pl.ds(EXT0 + t - HIST, HIST), :]
    xcb = xc.astype(BF16)
    r = jax.nn.sigmoid(_dot(xcb, wa_ref[...]) + lv_ref[1:2, :])
    gi = jax.nn.sigmoid(_dot(xcb, wx_ref[...]) + lv_ref[2:3, :])
    log_a = jnp.where(rowmask, -LRU_C * r * jax.nn.softplus(-lv_ref[3:4, :]), 0.0)
    a = jnp.exp(log_a)
    uu = jnp.where(rowmask, jnp.sqrt(1.0 - a * a) * (gi * xc), 0.0)
    rows = _iota2((t, 1), 0)
    sh = 1
    while sh < t:
        keep = rows >= sh
        uu = jnp.where(keep, a * pltpu.roll(uu, sh, 0) + uu, uu)
        a = jnp.where(keep, a * pltpu.roll(a, sh, 0), a)
        sh *= 2
    hh = uu + a * h_scr[...]
    h_scr[...] = hh[t - 1:t, :]
    yb = yb_ref[...]
    gelu = 0.5 * yb * (1.0 + jnp.tanh(math.sqrt(2.0 / math.pi) * (yb + 0.044715 * (yb * yb * yb))))
    mixed_ref[:, nqk:] = jnp.where(rowmask, gelu * hh, 0.0)

    @pl.when(c == c_last)
    def _():
        s_out[0] = s_scr[...]
        h_out[0] = h_scr[...]


def _ab_mixer(proj, dconv0, s0, lconv0, h0, wts, *, t, length, batch, cpb, rb0):
    dnw, dnv, lw, lv, wa, wx = wts
    heads = s0.shape[1]
    nqk = heads * LANE
    nv = heads * s0.shape[3]
    lw_ch = lw.shape[1]
    dmix = nv + lw_ch
    rmap = lambda j: (lambda b, c: (rb0 + b * cpb + c, j))
    bmap3 = lambda b, c: (b, 0, 0)
    bmap4 = lambda b, c: (b, 0, 0, 0)
    full2 = lambda b, c: (0, 0)
    in_specs = [
        pl.BlockSpec((t, 3 * nqk), rmap(0)),
        pl.BlockSpec((t, nv), rmap(3 * nqk // nv)),
        pl.BlockSpec((t, lw_ch), rmap((3 * nqk + nv) // lw_ch)),
        pl.BlockSpec((t, lw_ch), rmap((3 * nqk + nv) // lw_ch + 1)),
        pl.BlockSpec((t, 2 * LANE), rmap((3 * nqk + nv + 2 * lw_ch) // (2 * LANE))),
        pl.BlockSpec((1, HIST, 3 * nqk), bmap3),
        pl.BlockSpec((1,) + s0.shape[1:], bmap4),
        pl.BlockSpec((1, HIST, lw_ch), bmap3),
        pl.BlockSpec((1, 1, lw_ch), bmap3),
        pl.BlockSpec(dnw.shape, full2), pl.BlockSpec(dnv.shape, full2),
        pl.BlockSpec(lw.shape, full2), pl.BlockSpec(lv.shape, full2),
        pl.BlockSpec(wa.shape, full2), pl.BlockSpec(wx.shape, full2),
    ]
    args = [proj, proj, proj, proj, proj, dconv0, s0, lconv0, h0, dnw, dnv, lw, lv, wa, wx]
    out_shape = (jax.ShapeDtypeStruct((batch * cpb * t, dmix), F32),
                 jax.ShapeDtypeStruct(s0.shape, F32),
                 jax.ShapeDtypeStruct(dconv0.shape, F32),
                 jax.ShapeDtypeStruct(h0.shape, F32),
                 jax.ShapeDtypeStruct(lconv0.shape, F32))
    out_specs = (pl.BlockSpec((t, dmix), lambda b, c: (b * cpb + c, 0)),
                 pl.BlockSpec((1,) + s0.shape[1:], bmap4),
                 pl.BlockSpec((1, HIST, 3 * nqk), bmap3),
                 pl.BlockSpec((1, 1, lw_ch), bmap3),
                 pl.BlockSpec((1, HIST, lw_ch), bmap3))
    return pl.pallas_call(
        functools.partial(_ab_kernel, t=t, length=length, heads=heads),
        grid=(batch, cpb),
        in_specs=in_specs, out_specs=out_specs, out_shape=out_shape,
        scratch_shapes=[pltpu.VMEM((t + EXT0, 3 * nqk), F32), pltpu.VMEM((t + EXT0, lw_ch), F32),
                        pltpu.VMEM(s0.shape[1:], F32), pltpu.VMEM((1, lw_ch), F32)],
        compiler_params=pltpu.CompilerParams(dimension_semantics=("parallel", "arbitrary"),
                                             vmem_limit_bytes=VMEM_LIMIT),
        name="ab_mixer")(*args)


def _ssd_kernel(z_ref, xs_ref, bc_ref, dt_ref, xconv0_ref, bconv0_ref, s0_ref,
                wxs_ref, wbc_ref, bxs_ref, bbc_ref, vec_ref, ng_ref, *rest,
                t, length, heads, groups):
    mixed_ref, s_out, conv_out, xs_ext, bc_ext, s_scr, ybuf = rest
    c = pl.program_id(1)
    c_last = (length - 1) // t
    v_last = length - c_last * t
    inner = xs_ref.shape[1]
    p = inner // heads
    n = s_scr.shape[2]
    hpg = heads // groups
    gw = inner // groups
    cdt = BF16 if t % 16 == 0 else F32

    @pl.when(c == 0)
    def _():
        xs_ext[pl.ds(EXT0 - HIST, HIST), :] = xconv0_ref[0]
        bc_ext[pl.ds(EXT0 - HIST, HIST), :] = bconv0_ref[0]
        s_scr[...] = s0_ref[0]

    rowmask = _iota2((t, 1), 0) < (length - c * t)
    xs = _silu(_causal_conv(xs_ext, jnp.where(rowmask, xs_ref[...], 0.0), wxs_ref[...], t) + bxs_ref[...])
    bc = _silu(_causal_conv(bc_ext, jnp.where(rowmask, bc_ref[...], 0.0), wbc_ref[...], t) + bbc_ref[...])

    @pl.when(c == c_last)
    def _():
        conv_out[0, :, :inner] = xs_ext[pl.ds(EXT0 + v_last - HIST, HIST), :]
        conv_out[0, :, inner:] = bc_ext[pl.ds(EXT0 + v_last - HIST, HIST), :]

    xs_ext[pl.ds(EXT0 - HIST, HIST), :] = xs_ext[pl.ds(EXT0 + t - HIST, HIST), :]
    bc_ext[pl.ds(EXT0 - HIST, HIST), :] = bc_ext[pl.ds(EXT0 + t - HIST, HIST), :]

    dt = jnp.where(rowmask, jax.nn.softplus(jnp.where(rowmask, dt_ref[...], 0.0) + vec_ref[0:1, :]), 0.0)
    da = dt * (-jnp.exp(vec_ref[1:2, :]))
    d_skip = vec_ref[2:3, :]
    cs, cst = _cumsum_rows(da)
    ecs = jnp.exp(cs)
    e_last = jnp.exp(cs[t - 1:t, :])
    w_dec = jnp.exp(cs[t - 1:t, :] - cs)
    causal = _iota2((t, t), 0) >= _iota2((t, t), 1)
    gn = groups * n
    for g in range(groups):
        bm = bc[:, g * n:(g + 1) * n].astype(cdt)
        cm = bc[:, gn + g * n:gn + (g + 1) * n].astype(cdt)
        cb = _dot_nt(cm, bm)
        for j in range(hpg):
            e = g * hpg + j
            decay = jnp.where(causal, jnp.exp(cs[:, e:e + 1] - cst[e:e + 1, :]), 0.0)
            xe = xs[:, e * p:(e + 1) * p]
            xdt = xe * dt[:, e:e + 1]
            s = s_scr[e]
            y = _dot((cb * decay).astype(cdt), xdt.astype(cdt)) + _dot_nt(cm, s.astype(cdt)) * ecs[:, e:e + 1]
            s_scr[e] = s * e_last[:, e:e + 1] + _dot_tn((xdt * w_dec[:, e:e + 1]).astype(cdt), bm)
            ybuf[:, e * p:(e + 1) * p] = y + xe * d_skip[:, e:e + 1]
    yz = ybuf[...] * _silu(z_ref[...])
    ng = ng_ref[...]
    for g in range(groups):
        seg = yz[:, g * gw:(g + 1) * gw]
        seg = seg * lax.rsqrt(jnp.mean(seg * seg, axis=-1, keepdims=True) + RMS_EPS) * ng[:, g * gw:(g + 1) * gw]
        mixed_ref[:, g * gw:(g + 1) * gw] = jnp.where(rowmask, seg, 0.0)

    @pl.when(c == c_last)
    def _():
        s_out[0] = s_scr[...]


def _ssd_mixer(proj, conv0, s0, wts, *, t, length, batch, cpb, rb0, groups):
    cw, cb, vec, ng = wts
    heads = s0.shape[1]
    inner = heads * s0.shape[2]
    rmap = lambda j: (lambda b, c: (rb0 + b * cpb + c, j))
    bmap4 = lambda b, c: (b, 0, 0, 0)
    in_specs = [
        pl.BlockSpec((t, inner), rmap(0)), pl.BlockSpec((t, inner), rmap(1)), pl.BlockSpec((t, inner), rmap(2)),
        pl.BlockSpec((t, LANE), rmap(3 * inner // LANE)),
        pl.BlockSpec((1, HIST, inner), lambda b, c: (b, 0, 0)),
        pl.BlockSpec((1, HIST, inner), lambda b, c: (b, 0, 1)),
        pl.BlockSpec((1,) + s0.shape[1:], bmap4),
        pl.BlockSpec((CONV_W, inner), lambda b, c: (0, 0)), pl.BlockSpec((CONV_W, inner), lambda b, c: (0, 1)),
        pl.BlockSpec((1, inner), lambda b, c: (0, 0)), pl.BlockSpec((1, inner), lambda b, c: (0, 1)),
        pl.BlockSpec(vec.shape, lambda b, c: (0, 0)), pl.BlockSpec(ng.shape, lambda b, c: (0, 0)),
    ]
    args = [proj, proj, proj, proj, conv0, conv0, s0, cw, cw, cb, cb, vec, ng]
    out_shape = (jax.ShapeDtypeStruct((batch * cpb * t, inner), F32),
                 jax.ShapeDtypeStruct(s0.shape, F32),
                 jax.ShapeDtypeStruct(conv0.shape, F32))
    out_specs = (pl.BlockSpec((t, inner), lambda b, c: (b * cpb + c, 0)),
                 pl.BlockSpec((1,) + s0.shape[1:], bmap4),
                 pl.BlockSpec((1, HIST, conv0.shape[2]), lambda b, c: (b, 0, 0)))
    return pl.pallas_call(
        functools.partial(_ssd_kernel, t=t, length=length, heads=heads, groups=groups),
        grid=(batch, cpb),
        in_specs=in_specs, out_specs=out_specs, out_shape=out_shape,
        scratch_shapes=[pltpu.VMEM((t + EXT0, inner), F32), pltpu.VMEM((t + EXT0, inner), F32),
                        pltpu.VMEM(s0.shape[1:], F32), pltpu.VMEM((t, inner), F32)],
        compiler_params=pltpu.CompilerParams(dimension_semantics=("parallel", "arbitrary"),
                                             vmem_limit_bytes=VMEM_LIMIT),
        name="ssd_mixer")(*args)


def _layer_norm(h, g, b):
    mu = jnp.mean(h, axis=-1, keepdims=True)
    d = h - mu
    var = jnp.mean(d * d, axis=-1, keepdims=True)
    return d * lax.rsqrt(var + LN_EPS) * g + b


def _col_to_rows(v):
    eye = _iota2((LANE, LANE), 0) == _iota2((LANE, LANE), 1)
    rows = [jnp.sum(jnp.where(eye, v[i * LANE:(i + 1) * LANE, :], 0.0), axis=0, keepdims=True)
            for i in range(v.shape[0] // LANE)]
    return jnp.concatenate(rows, axis=0)


def _outln_kernel(yp_ref, ys_ref, x_ref, w_ref, ln_ref, wr_ref, br_ref, o_ref, bkt_ref, rank_ref, cnt_ref,
                  ybuf, cnt_scr, *, alpha, n_groups, epg, nbp, n_buckets):
    d = x_ref.shape[1]
    i = pl.program_id(0)

    @pl.when(i == 0)
    def _():
        cnt_scr[...] = jnp.zeros_like(cnt_scr)

    @pl.when(i < nbp)
    def _():
        ybuf[...] = yp_ref[...].astype(BF16)

    @pl.when(i >= nbp)
    def _():
        ybuf[...] = ys_ref[...].astype(BF16)

    acc = _dot(ybuf[...], w_ref[...])
    xn = _layer_norm(alpha * x_ref[...] + acc, ln_ref[0:1, :], ln_ref[1:2, :])
    o_ref[:, :d] = xn
    logits = _dot(xn, wr_ref[...], precision=HI) + br_ref[...]
    tm = logits.shape[0]
    lane = _iota2((tm, ROUTE_LANES), 1)
    ninf = -jnp.inf
    gl = jnp.where(lane < n_groups, logits, ninf)
    gmax = jnp.max(gl, axis=-1, keepdims=True)
    gidx = jnp.min(jnp.where(gl == gmax, lane, ROUTE_LANES), axis=-1, keepdims=True)
    g_w = 1.0 / jnp.sum(jnp.where(lane < n_groups, jnp.exp(logits - gmax), 0.0), axis=-1, keepdims=True)
    lo = n_groups + epg * gidx
    el = jnp.where((lane >= lo) & (lane < lo + epg), logits, ninf)
    m1 = jnp.max(el, axis=-1, keepdims=True)
    i1 = jnp.min(jnp.where(el == m1, lane, ROUTE_LANES), axis=-1, keepdims=True)
    el2 = jnp.where(lane == i1, ninf, el)
    m2 = jnp.max(el2, axis=-1, keepdims=True)
    i2 = jnp.min(jnp.where(el2 == m2, lane, ROUTE_LANES), axis=-1, keepdims=True)
    r = jnp.exp(m2 - m1)
    w1 = g_w / (1.0 + r)
    w2 = g_w * r / (1.0 + r)
    e1 = i1 - lo
    e2 = i2 - lo
    first = e1 < e2
    ea = jnp.where(first, e1, e2)
    eb = jnp.where(first, e2, e1)
    bucket = gidx * PAIR_SLOTS + ea * epg + eb
    ga = jnp.where(first, w1, w2)
    gb = jnp.where(first, w2, w1)
    o_ref[:, d:] = jnp.where(lane == 1, ga, jnp.where(lane == 2, gb, 0.0))
    onehot = _iota2((tm, n_buckets), 1) == bucket
    oh = jnp.where(onehot, 1.0, 0.0)
    earlier = jnp.where(_iota2((tm, tm), 0) > _iota2((tm, tm), 1), 1.0, 0.0).astype(BF16)
    before = _dot(earlier, oh.astype(BF16)) + cnt_scr[...]
    rank = jnp.sum(jnp.where(onehot, before, 0.0), axis=-1, keepdims=True)
    cnt_scr[...] = cnt_scr[...] + jnp.sum(oh, axis=0, keepdims=True)
    bkt_ref[0] = _col_to_rows(bucket.astype(F32)).astype(jnp.int32)
    rank_ref[0] = _col_to_rows(rank).astype(jnp.int32)
    cnt_ref[...] = jnp.broadcast_to(cnt_scr[...], cnt_ref.shape).astype(jnp.int32)


def _outln(mixed_p, mixed_s, x, w, ln, wr, br, *, n, tm, alpha, n_groups, epg, n_buckets):
    d = x.shape[1]
    dm = mixed_p.shape[1]
    nbp = mixed_p.shape[0] // tm
    nb = n // tm
    return pl.pallas_call(
        functools.partial(_outln_kernel, alpha=alpha, n_groups=n_groups, epg=epg, nbp=nbp, n_buckets=n_buckets),
        grid=(nb,),
        in_specs=[pl.BlockSpec((tm, dm), lambda i: (jnp.minimum(i, nbp - 1), 0)),
                  pl.BlockSpec((tm, dm), lambda i: (jnp.maximum(i - nbp, 0), 0)),
                  pl.BlockSpec((tm, d), lambda i: (i, 0)),
                  pl.BlockSpec((dm, d), lambda i: (0, 0), pipeline_mode=pl.Buffered(1)),
                  pl.BlockSpec(ln.shape, lambda i: (0, 0)),
                  pl.BlockSpec(wr.shape, lambda i: (0, 0)),
                  pl.BlockSpec(br.shape, lambda i: (0, 0))],
        out_specs=(pl.BlockSpec((tm, d + ROUTE_LANES), lambda i: (i, 0)),
                   pl.BlockSpec((1, tm // LANE, LANE), lambda i: (i, 0, 0)),
                   pl.BlockSpec((1, tm // LANE, LANE), lambda i: (i, 0, 0)),
                   pl.BlockSpec((8, n_buckets), lambda i: (0, 0))),
        out_shape=(jax.ShapeDtypeStruct((n, d + ROUTE_LANES), F32),
                   jax.ShapeDtypeStruct((nb, tm // LANE, LANE), jnp.int32),
                   jax.ShapeDtypeStruct((nb, tm // LANE, LANE), jnp.int32),
                   jax.ShapeDtypeStruct((8, n_buckets), jnp.int32)),
        scratch_shapes=[pltpu.VMEM((tm, dm), BF16), pltpu.VMEM((1, n_buckets), F32)],
        compiler_params=pltpu.CompilerParams(dimension_semantics=("arbitrary",), vmem_limit_bytes=VMEM_LIMIT),
        name="outproj_ln_route")(mixed_p, mixed_s, x, w, ln, wr, br)


def _moe_kernel(valid_ref, ea_ref, eb_ref, src_ref, nxt_ref, x_hbm, w1a_ref, w3a_ref, w2a_ref, w1b_ref, w3b_ref,
                w2b_ref, ln_ref, o_hbm, xbuf, obuf, gsem, ssem, *, alpha, tr, n_rows, unroll):
    t = pl.program_id(0)
    nt = pl.num_programs(0)
    d = obuf.shape[2]
    slot = lax.rem(t, 2)
    other = 1 - slot
    v_t = valid_ref[t] > 0
    v_next = jnp.logical_and(t + 1 < nt, valid_ref[jnp.minimum(t + 1, nt - 1)] > 0)

    def gather(idx_ref, s):
        def body(i, carry):
            for u in range(unroll):
                r = i * unroll + u
                pltpu.make_async_copy(x_hbm.at[pl.ds(jnp.minimum(idx_ref[0, 0, r], n_rows - 1), 1), :],
                                      xbuf.at[s, pl.ds(r, 1), :], gsem.at[s]).start()
            return carry
        lax.fori_loop(0, tr // unroll, body, 0)

    def wait_gather(s):
        pltpu.make_async_copy(x_hbm.at[pl.ds(0, tr), :], xbuf.at[s], gsem.at[s]).wait()

    def scatter(s):
        def body(i, carry):
            for u in range(unroll):
                r = i * unroll + u
                pltpu.make_async_copy(obuf.at[s, pl.ds(r, 1), :],
                                      o_hbm.at[pl.ds(src_ref[0, 0, r], 1), :], ssem.at[s]).start()
            return carry
        lax.fori_loop(0, tr // unroll, body, 0)

    def wait_scatter(s):
        pltpu.make_async_copy(obuf.at[s], o_hbm.at[pl.ds(0, tr), :], ssem.at[s]).wait()

    @pl.when(t == 0)
    def _():
        obuf[...] = jnp.zeros_like(obuf)
        for s in range(2):
            fill = pltpu.make_async_copy(obuf.at[s], o_hbm.at[pl.ds(n_rows + s * tr, tr), :], ssem.at[s])
            fill.start()
            fill.wait()

    @pl.when(jnp.logical_and(t == 0, v_t))
    def _():
        gather(src_ref, 0)

    @pl.when(v_next)
    def _():
        gather(nxt_ref, other)

    @pl.when(v_t)
    def _():
        wait_gather(slot)
        x = xbuf[slot, :, :d]
        ga = xbuf[slot, :, d + 1:d + 2]
        gb = xbuf[slot, :, d + 2:d + 3]
        xb = x.astype(BF16)
        ha = _silu(_dot(xb, w1a_ref[0])) * _dot(xb, w3a_ref[0])
        hb = _silu(_dot(xb, w1b_ref[0])) * _dot(xb, w3b_ref[0])
        ffn = _dot((ha * ga).astype(BF16), w2a_ref[0]) + _dot((hb * gb).astype(BF16), w2b_ref[0])
        y = _layer_norm(alpha * x + ffn, ln_ref[0:1, :], ln_ref[1:2, :])

        @pl.when(t >= 2)
        def _():
            wait_scatter(slot)

        obuf[slot] = y
        scatter(slot)

        @pl.when(jnp.logical_not(v_next))
        def _():
            wait_scatter(slot)

            @pl.when(t >= 1)
            def _():
                wait_scatter(other)


def _moe(x1r, src, valid, ea, eb, w1, w3, w2, ln, *, alpha, tr):
    n, dr = x1r.shape
    d = dr - ROUTE_LANES
    nt = valid.shape[0]
    f = w1.shape[2]
    wa = lambda t, v, a, b: (a[t], 0, 0)
    wb = lambda t, v, a, b: (b[t], 0, 0)
    grid_spec = pltpu.PrefetchScalarGridSpec(
        num_scalar_prefetch=3,
        grid=(nt,),
        in_specs=[pl.BlockSpec((1, 1, tr), lambda t, v, a, b: (t, 0, 0), memory_space=pltpu.SMEM),
                  pl.BlockSpec((1, 1, tr), lambda t, v, a, b: (jnp.minimum(t + 1, nt - 1), 0, 0),
                               memory_space=pltpu.SMEM),
                  pl.BlockSpec(memory_space=pl.ANY),
                  pl.BlockSpec((1, d, f), wa), pl.BlockSpec((1, d, f), wa), pl.BlockSpec((1, f, d), wa),
                  pl.BlockSpec((1, d, f), wb), pl.BlockSpec((1, d, f), wb), pl.BlockSpec((1, f, d), wb),
                  pl.BlockSpec(ln.shape, lambda t, v, a, b: (0, 0))],
        out_specs=pl.BlockSpec(memory_space=pl.ANY),
        scratch_shapes=[pltpu.VMEM((2, tr, dr), F32), pltpu.VMEM((2, tr, d), F32),
                        pltpu.SemaphoreType.DMA((2,)), pltpu.SemaphoreType.DMA((2,))])
    return pl.pallas_call(
        functools.partial(_moe_kernel, alpha=alpha, tr=tr, n_rows=n, unroll=8),
        grid_spec=grid_spec,
        out_shape=jax.ShapeDtypeStruct((n + 2 * tr, d), F32),
        compiler_params=pltpu.CompilerParams(dimension_semantics=("arbitrary",), vmem_limit_bytes=VMEM_LIMIT),
        name="moe_routed")(valid, ea, eb, src, src, x1r, w1, w3, w2, w1, w3, w2, ln)


def _route_schedule(bucket, rank, counts, *, n, tr, nt, epg, e_off):
    nbk = counts.shape[0]
    tiles_b = (counts + tr - 1) // tr
    tile_end = jnp.cumsum(tiles_b)
    pstart = (tile_end - tiles_b) * tr
    hit = bucket[:, None] == jnp.arange(nbk, dtype=jnp.int32)[None, :]
    dest = jnp.sum(jnp.where(hit, pstart[None, :], 0), axis=1) + rank
    slots = jnp.arange(nt * tr, dtype=jnp.int32)
    trash = n + ((slots // tr) % 2) * tr + slots % tr
    src = trash.at[dest].set(jnp.arange(n, dtype=jnp.int32))
    tiles = jnp.arange(nt, dtype=jnp.int32)
    valid = tiles < tile_end[-1]
    tb = jnp.sum(tile_end[None, :] <= jnp.minimum(tiles, tile_end[-1] - 1)[:, None], axis=1).astype(jnp.int32)
    grp = tb // PAIR_SLOTS
    pair = tb % PAIR_SLOTS
    ea = e_off + grp * epg + pair // epg
    eb = e_off + grp * epg + pair % epg
    return src.reshape(nt, 1, tr), valid.astype(jnp.int32), ea.astype(jnp.int32), eb.astype(jnp.int32)


def _pad_cols(w, width):
    return jnp.pad(w, ((0, 0), (0, width - w.shape[1])))


def _pad_vec(v, width=LANE):
    return jnp.pad(v.astype(F32), (0, width - v.shape[0]))[None, :]


def _rows8(*vecs):
    m = jnp.stack([v.astype(F32) for v in vecs], axis=0)
    return jnp.pad(m, ((0, 8 - m.shape[0]), (0, 0)))


def _block_diag(w):
    nb, bd, _ = w.shape
    eye = jnp.eye(nb, dtype=w.dtype)
    return jnp.einsum("nde,nm->ndme", w, eye).reshape(nb * bd, nb * bd)


def kernel(x_prompt, x_sample, state_delta, state_delta_conv, state_lru, state_lru_conv, state_ssm, state_ssm_conv,
           meta, w_in_ab, dn_conv_w, dn_a_log, dn_dt_bias, dn_norm_g, lru_conv_w, lru_conv_b, lru_w_a, lru_b_a,
           lru_w_x, lru_b_x, lru_lam, w_out_ab, w_in_ssd, ssd_conv_w, ssd_conv_b, ssd_dt_bias, ssd_a_log, ssd_d,
           ssd_norm_g, w_out_ssd, ln_g, ln_b, moe_w_group, moe_b_group, moe_w_expert, moe_b_expert, moe_w1, moe_w3,
           moe_w2):
    bp, seq, d = x_prompt.shape
    bs, ls, _ = x_sample.shape
    depth = ln_g.shape[0]
    alpha = (2.0 * depth) ** 0.25
    n_groups, epg = moe_w1.shape[1], moe_w1.shape[2]
    heads_dn = dn_a_log.shape[1]
    nqk = heads_dn * state_delta.shape[3]
    nv = heads_dn * state_delta.shape[4]
    lru_w = lru_lam.shape[1]
    heads_ssd = ssd_a_log.shape[1]
    inner = heads_ssd * state_ssm.shape[3]
    n_state = state_ssm.shape[4]
    ssd_groups = (ssd_conv_w.shape[2] - inner) // (2 * n_state)
    assert nqk == nv and state_delta.shape[3] == LANE and state_delta.shape[4] == LANE and n_state == LANE

    lp_len = N_META + seq
    chunk_lcm = math.lcm(DN_CHUNK, SSD_CHUNK)
    lpad = -(-lp_len // chunk_lcm) * chunk_lcm
    np_rows = bp * lpad
    ns_rows = bs * ls
    ntot = np_rows + ns_rows
    tm = math.gcd(math.gcd(np_rows, ns_rows), ROW_TILE)
    assert tm % LANE == 0 and np_rows % ls == 0 and ls % 8 == 0 and ls >= HIST
    assert (lp_len - 1) % DN_CHUNK + 1 >= HIST and (lp_len - 1) % SSD_CHUNK + 1 >= HIST

    xp = jnp.concatenate([jnp.broadcast_to(meta.astype(F32), (bp, N_META, d)), x_prompt,
                          jnp.zeros((bp, lpad - lp_len, d), F32)], axis=1)
    x = jnp.concatenate([xp.reshape(np_rows, d), x_sample.reshape(ns_rows, d)], axis=0)

    tr = MOE_TILE
    n_buckets = n_groups * PAIR_SLOTS
    n_pairs = n_groups * (epg * (epg - 1) // 2)
    nt = -(-(ntot + n_pairs * (tr - 1)) // tr)
    n_exp = n_groups * epg
    w1 = moe_w1.reshape(depth * n_exp, d, -1).astype(BF16)
    w3 = moe_w3.reshape(depth * n_exp, d, -1).astype(BF16)
    w2 = moe_w2.reshape(depth * n_exp, -1, d).astype(BF16)

    zeros = lambda *s: jnp.zeros(s, F32)
    outs = {k: [] for k in ("pd", "pdc", "pl", "plc", "ps", "psc", "sd", "sdc", "sl", "slc", "ss", "ssc")}
    for layer in range(depth):
        i = layer // 2
        if layer % 2 == 0:
            wi = w_in_ab[i]
            o_b = 3 * nqk + nv
            w_in = jnp.concatenate([wi[:, :o_b], wi[:, o_b + 2 * heads_dn:],
                                    _pad_cols(wi[:, o_b:o_b + heads_dn], LANE),
                                    _pad_cols(wi[:, o_b + heads_dn:o_b + 2 * heads_dn], LANE)], axis=1).astype(BF16)
            proj = _inproj(x, w_in, tm, ntot)
            wts = (dn_conv_w[i], _rows8(_pad_vec(dn_a_log[i])[0], _pad_vec(dn_dt_bias[i])[0], dn_norm_g[i]),
                   lru_conv_w[i], _rows8(lru_conv_b[i], lru_b_a[i], lru_b_x[i], lru_lam[i]),
                   _block_diag(lru_w_a[i]).astype(BF16), _block_diag(lru_w_x[i]).astype(BF16))
            mixed_p, s_p, dc_p, h_p, lc_p = _ab_mixer(
                proj, zeros(bp, HIST, 3 * nqk), zeros(bp, *state_delta.shape[2:]), zeros(bp, HIST, lru_w),
                zeros(bp, 1, lru_w), wts, t=DN_CHUNK, length=lp_len, batch=bp, cpb=lpad // DN_CHUNK, rb0=0)
            mixed_s, s_s, dc_s, h_s, lc_s = _ab_mixer(
                proj, state_delta_conv[i], state_delta[i], state_lru_conv[i], state_lru[i][:, None, :], wts,
                t=ls, length=ls, batch=bs, cpb=1, rb0=np_rows // ls)
            outs["pd"].append(s_p); outs["pdc"].append(dc_p); outs["pl"].append(h_p[:, 0]); outs["plc"].append(lc_p)
            outs["sd"].append(s_s); outs["sdc"].append(dc_s); outs["sl"].append(h_s[:, 0]); outs["slc"].append(lc_s)
            w_out = w_out_ab[i].astype(BF16)
        else:
            wi = w_in_ssd[i]
            e_raw = wi.shape[1]
            w_in = _pad_cols(wi, -(-e_raw // LANE) * LANE).astype(BF16)
            proj = _inproj(x, w_in, tm, ntot)
            wts = (ssd_conv_w[i], ssd_conv_b[i][None, :],
                   _rows8(_pad_vec(ssd_dt_bias[i])[0], _pad_vec(ssd_a_log[i])[0], _pad_vec(ssd_d[i])[0]),
                   ssd_norm_g[i][None, :])
            mixed_p, s_p, c_p = _ssd_mixer(
                proj, zeros(bp, HIST, ssd_conv_w.shape[2]), zeros(bp, *state_ssm.shape[2:]), wts,
                t=SSD_CHUNK, length=lp_len, batch=bp, cpb=lpad // SSD_CHUNK, rb0=0, groups=ssd_groups)
            mixed_s, s_s, c_s = _ssd_mixer(
                proj, state_ssm_conv[i], state_ssm[i], wts,
                t=ls, length=ls, batch=bs, cpb=1, rb0=np_rows // ls, groups=ssd_groups)
            outs["ps"].append(s_p); outs["psc"].append(c_p)
            outs["ss"].append(s_s); outs["ssc"].append(c_s)
            w_out = w_out_ssd[i].astype(BF16)
        wr = _pad_cols(jnp.concatenate([moe_w_group[layer], moe_w_expert[layer]], axis=1), ROUTE_LANES)
        br = _pad_vec(jnp.concatenate([moe_b_group[layer], moe_b_expert[layer]]), ROUTE_LANES)
        x1r, bucket, rank, counts = _outln(mixed_p, mixed_s, x, w_out, _rows8(ln_g[layer, 0], ln_b[layer, 0]), wr, br,
                                           n=ntot, tm=tm, alpha=alpha, n_groups=n_groups, epg=epg,
                                           n_buckets=n_buckets)
        src, valid, ea, eb = _route_schedule(bucket.reshape(ntot), rank.reshape(ntot), counts[0], n=ntot, tr=tr,
                                             nt=nt, epg=epg, e_off=layer * n_exp)
        x = _moe(x1r, src, valid, ea, eb, w1, w3, w2, _rows8(ln_g[layer, 1], ln_b[layer, 1]), alpha=alpha, tr=tr)

    y_prompt = x[:np_rows].reshape(bp, lpad, d)[:, N_META:lp_len]
    y_sample = x[np_rows:ntot].reshape(bs, ls, d)
    st = lambda k: jnp.stack(outs[k], axis=0)
    return (y_prompt, y_sample, st("pd"), st("pdc"), st("pl"), st("plc"), st("ps"), st("psc"),
            st("sd"), st("sdc"), st("sl"), st("slc"), st("ss"), st("ssc"))
```

```python
import functools
import math

import jax
import jax.numpy as jnp
from jax import lax
from jax.experimental import pallas as pl
from jax.experimental.pallas import tpu as pltpu

F32 = jnp.float32
BF16 = jnp.bfloat16
HI = lax.Precision.HIGHEST

LN_EPS = 1e-5
RMS_EPS = 1e-6
LRU_C = 8.0
N_META = 16
CONV_W = 4
HIST = CONV_W - 1

LANE = 128
EXT0 = 8
DN_CHUNK = 64
SSD_CHUNK = 128
ROW_TILE = 512
MOE_TILE = 128
PAIR_SLOTS = 64
ROUTE_LANES = 128
VMEM_LIMIT = 56 * 1024 * 1024


def _silu(x):
    return x * jax.nn.sigmoid(x)


def _dot(a, b, **kw):
    return jnp.dot(a, b, preferred_element_type=F32, **kw)


def _dot_nt(a, b, **kw):
    return lax.dot_general(a, b, (((1,), (1,)), ((), ())), preferred_element_type=F32, **kw)


def _dot_tn(a, b, **kw):
    return lax.dot_general(a, b, (((0,), (0,)), ((), ())), preferred_element_type=F32, **kw)


def _iota2(shape, dim):
    return lax.broadcasted_iota(jnp.int32, shape, dim)


def _cumsum_rows(x):
    t = x.shape[0]
    rows = _iota2((t, 1), 0)
    cs = x
    sh = 1
    while sh < t:
        cs = cs + jnp.where(rows >= sh, pltpu.roll(cs, sh, 0), 0.0)
        sh *= 2
    if t == LANE:
        cst = cs.T
    else:
        eye = (_iota2((LANE, LANE), 0) == _iota2((LANE, LANE), 1)).astype(F32)
        cst = _dot_nt(eye, cs, precision=HI)
    return cs, cst


def _expand_heads(v, e0, nh, p):
    lane = _iota2((v.shape[0], nh * p), 1)
    out = jnp.broadcast_to(v[:, e0:e0 + 1], (v.shape[0], nh * p))
    for j in range(1, nh):
        out = jnp.where(lane >= j * p, v[:, e0 + j:e0 + j + 1], out)
    return out


def _causal_conv(ext_ref, x, w, t):
    ext_ref[pl.ds(EXT0, t), :] = x
    y = ext_ref[pl.ds(EXT0 - HIST, t), :] * w[0:1, :]
    for k in range(1, CONV_W):
        y = y + ext_ref[pl.ds(EXT0 - HIST + k, t), :] * w[k:k + 1, :]
    return y


def _inproj_kernel(x_ref, w_ref, o_ref, *, tn):
    xb = x_ref[...].astype(BF16)
    for j in range(o_ref.shape[1] // tn):
        o_ref[:, j * tn:(j + 1) * tn] = _dot(xb, w_ref[:, j * tn:(j + 1) * tn])


def _inproj(x, w, tm, n):
    k = x.shape[1]
    e = w.shape[1]
    tn = max(c for c in range(LANE, 1024 + LANE, LANE) if e % c == 0)
    return pl.pallas_call(
        functools.partial(_inproj_kernel, tn=tn),
        grid=(n // tm,),
        in_specs=[pl.BlockSpec((tm, k), lambda i: (i, 0)),
                  pl.BlockSpec((k, e), lambda i: (0, 0), pipeline_mode=pl.Buffered(1))],
        out_specs=pl.BlockSpec((tm, e), lambda i: (i, 0)),
        out_shape=jax.ShapeDtypeStruct((n, e), F32),
        compiler_params=pltpu.CompilerParams(dimension_semantics=("parallel",), vmem_limit_bytes=VMEM_LIMIT),
        name="inproj")(x, w)


def _ab_kernel(qkv_ref, z_ref, xb_ref, yb_ref, bg_ref, dconv0_ref, s0_ref, lconv0_ref, h0_ref,
               dnw_ref, dnv_ref, lw_ref, lv_ref, wa_ref, wx_ref, *rest,
               t, length, heads):
    mixed_ref, s_out, dconv_out, h_out, lconv_out, qkv_ext, xb_ext, s_scr, h_scr = rest
    c = pl.program_id(1)
    c_last = (length - 1) // t
    v_last = length - c_last * t
    dk = LANE
    nqk = heads * dk
    cdt = BF16 if t % 16 == 0 else F32

    @pl.when(c == 0)
    def _():
        qkv_ext[pl.ds(EXT0 - HIST, HIST), :] = dconv0_ref[0]
        xb_ext[pl.ds(EXT0 - HIST, HIST), :] = lconv0_ref[0]
        s_scr[...] = s0_ref[0]
        h_scr[...] = h0_ref[0]

    rowmask = _iota2((t, 1), 0) < (length - c * t)

    x = jnp.where(rowmask, qkv_ref[...], 0.0)
    qkv = _silu(_causal_conv(qkv_ext, x, dnw_ref[...], t))

    @pl.when(c == c_last)
    def _():
        dconv_out[0] = qkv_ext[pl.ds(EXT0 + v_last - HIST, HIST), :]

    qkv_ext[pl.ds(EXT0 - HIST, HIST), :] = qkv_ext[pl.ds(EXT0 + t - HIST, HIST), :]

    bg = jnp.where(rowmask, bg_ref[...], 0.0)
    a_log = dnv_ref[0:1, :]
    dt_bias = dnv_ref[1:2, :]
    norm_g = dnv_ref[2:3, :]
    beta = jnp.where(rowmask, jax.nn.sigmoid(bg[:, :LANE]), 0.0)
    g = jnp.where(rowmask, -jnp.exp(a_log) * jax.nn.softplus(bg[:, LANE:] + dt_bias), 0.0)
    rows = _iota2((t, 1), 0)
    gc = g
    sh = 1
    while sh < t:
        gc = gc + jnp.where(rows >= sh, pltpu.roll(gc, sh, 0), 0.0)
        sh *= 2
    r = heads * t
    stack = lambda f: jnp.concatenate([f(h) for h in range(heads)], axis=0)
    l2n = lambda a: a * lax.rsqrt(jnp.sum(a * a, axis=-1, keepdims=True) + RMS_EPS)
    q = stack(lambda h: l2n(qkv[:, h * dk:(h + 1) * dk]) * (dk ** -0.5))
    k = stack(lambda h: l2n(qkv[:, nqk + h * dk:nqk + (h + 1) * dk]))
    v = stack(lambda h: qkv[:, 2 * nqk + h * dk:2 * nqk + (h + 1) * dk])
    beta_c = stack(lambda h: beta[:, h:h + 1])
    gc_c = stack(lambda h: gc[:, h:h + 1])
    row = _iota2((r, r), 0)
    col = _iota2((r, r), 1)
    gc_r = jnp.sum(jnp.where(row == col, gc_c, 0.0), axis=0, keepdims=True)
    same = (row // t) == (col // t)
    decay = jnp.where(same & (row >= col), jnp.exp(gc_c - gc_r), 0.0)
    eg = jnp.exp(gc_c)
    kb = k * beta_c
    kc = k.astype(cdt)
    a_mat = jnp.where(same & (row > col), _dot_nt(kb.astype(cdt), kc) * decay, 0.0)
    y = jnp.concatenate([v * beta_c, kb * eg], axis=-1)
    p = -a_mat
    nsteps = max(1, (t - 1).bit_length())
    for i in range(nsteps):
        pc = p.astype(cdt)
        y = y + _dot(pc, y.astype(cdt))
        if i + 1 < nsteps:
            p = _dot(pc, pc)
    u = y[:, :dk]
    w = y[:, dk:]
    qe = q * eg
    ws_qs = [_dot(jnp.concatenate([w[h * t:(h + 1) * t], qe[h * t:(h + 1) * t]], axis=0).astype(cdt),
                  s_scr[h].astype(cdt)) for h in range(heads)]
    v_new = u - jnp.concatenate([a[:t] for a in ws_qs], axis=0)
    attn = _dot_nt(q.astype(cdt), kc) * decay
    o = jnp.concatenate([a[t:] for a in ws_qs], axis=0) + _dot(attn.astype(cdt), v_new.astype(cdt))
    o = o * lax.rsqrt(jnp.mean(o * o, axis=-1, keepdims=True) + RMS_EPS) * norm_g
    z = z_ref[...]
    for h in range(heads):
        g_last = gc[t - 1:t, h:h + 1]
        k_dec = (k[h * t:(h + 1) * t] * jnp.exp(g_last - gc[:, h:h + 1])).astype(cdt)
        s_scr[h] = s_scr[h] * jnp.exp(g_last) + _dot_tn(k_dec, v_new[h * t:(h + 1) * t].astype(cdt))
        mixed_ref[:, h * dk:(h + 1) * dk] = jnp.where(rowmask, o[h * t:(h + 1) * t] * _silu(z[:, h * dk:(h + 1) * dk]), 0.0)

    xb = jnp.where(rowmask, xb_ref[...], 0.0)
    xc = _causal_conv(xb_ext, xb, lw_ref[...], t) + lv_ref[0:1, :]

    @pl.when(c == c_last)
    def _():
        lconv_out[0] = xb_ext[pl.ds(EXT0 + v_last - HIST, HIST), :]

    xb_ext[pl.ds(EXT0 - HIST, HIST), :] = xb_ext[pl.ds(EXT0 + t - HIST, HIST), :]
    xcb = xc.astype(BF16)
    r = jax.nn.sigmoid(_dot(xcb, wa_ref[...]) + lv_ref[1:2, :])
    gi = jax.nn.sigmoid(_dot(xcb, wx_ref[...]) + lv_ref[2:3, :])
    log_a = jnp.where(rowmask, -LRU_C * r * jax.nn.softplus(-lv_ref[3:4, :]), 0.0)
    a = jnp.exp(log_a)
    uu = jnp.where(rowmask, jnp.sqrt(1.0 - a * a) * (gi * xc), 0.0)
    rows = _iota2((t, 1), 0)
    sh = 1
    while sh < t:
        keep = rows >= sh
        uu = jnp.where(keep, a * pltpu.roll(uu, sh, 0) + uu, uu)
        a = jnp.where(keep, a * pltpu.roll(a, sh, 0), a)
        sh *= 2
    hh = uu + a * h_scr[...]
    h_scr[...] = hh[t - 1:t, :]
    yb = yb_ref[...]
    gelu = 0.5 * yb * (1.0 + jnp.tanh(math.sqrt(2.0 / math.pi) * (yb + 0.044715 * (yb * yb * yb))))
    mixed_ref[:, nqk:] = jnp.where(rowmask, gelu * hh, 0.0)

    @pl.when(c == c_last)
    def _():
        s_out[0] = s_scr[...]
        h_out[0] = h_scr[...]


def _ab_mixer(proj, dconv0, s0, lconv0, h0, wts, *, t, length, batch, cpb, rb0):
    dnw, dnv, lw, lv, wa, wx = wts
    heads = s0.shape[1]
    nqk = heads * LANE
    nv = heads * s0.shape[3]
    lw_ch = lw.shape[1]
    dmix = nv + lw_ch
    rmap = lambda j: (lambda b, c: (rb0 + b * cpb + c, j))
    bmap3 = lambda b, c: (b, 0, 0)
    bmap4 = lambda b, c: (b, 0, 0, 0)
    full2 = lambda b, c: (0, 0)
    in_specs = [
        pl.BlockSpec((t, 3 * nqk), rmap(0)),
        pl.BlockSpec((t, nv), rmap(3 * nqk // nv)),
        pl.BlockSpec((t, lw_ch), rmap((3 * nqk + nv) // lw_ch)),
        pl.BlockSpec((t, lw_ch), rmap((3 * nqk + nv) // lw_ch + 1)),
        pl.BlockSpec((t, 2 * LANE), rmap((3 * nqk + nv + 2 * lw_ch) // (2 * LANE))),
        pl.BlockSpec((1, HIST, 3 * nqk), bmap3),
        pl.BlockSpec((1,) + s0.shape[1:], bmap4),
        pl.BlockSpec((1, HIST, lw_ch), bmap3),
        pl.BlockSpec((1, 1, lw_ch), bmap3),
        pl.BlockSpec(dnw.shape, full2), pl.BlockSpec(dnv.shape, full2),
        pl.BlockSpec(lw.shape, full2), pl.BlockSpec(lv.shape, full2),
        pl.BlockSpec(wa.shape, full2), pl.BlockSpec(wx.shape, full2),
    ]
    args = [proj, proj, proj, proj, proj, dconv0, s0, lconv0, h0, dnw, dnv, lw, lv, wa, wx]
    out_shape = (jax.ShapeDtypeStruct((batch * cpb * t, dmix), F32),
                 jax.ShapeDtypeStruct(s0.shape, F32),
                 jax.ShapeDtypeStruct(dconv0.shape, F32),
                 jax.ShapeDtypeStruct(h0.shape, F32),
                 jax.ShapeDtypeStruct(lconv0.shape, F32))
    out_specs = (pl.BlockSpec((t, dmix), lambda b, c: (b * cpb + c, 0)),
                 pl.BlockSpec((1,) + s0.shape[1:], bmap4),
                 pl.BlockSpec((1, HIST, 3 * nqk), bmap3),
                 pl.BlockSpec((1, 1, lw_ch), bmap3),
                 pl.BlockSpec((1, HIST, lw_ch), bmap3))
    return pl.pallas_call(
        functools.partial(_ab_kernel, t=t, length=length, heads=heads),
        grid=(batch, cpb),
        in_specs=in_specs, out_specs=out_specs, out_shape=out_shape,
        scratch_shapes=[pltpu.VMEM((t + EXT0, 3 * nqk), F32), pltpu.VMEM((t + EXT0, lw_ch), F32),
                        pltpu.VMEM(s0.shape[1:], F32), pltpu.VMEM((1, lw_ch), F32)],
        compiler_params=pltpu.CompilerParams(dimension_semantics=("parallel", "arbitrary"),
                                             vmem_limit_bytes=VMEM_LIMIT),
        name="ab_mixer")(*args)


def _ssd_kernel(z_ref, xs_ref, bc_ref, dt_ref, xconv0_ref, bconv0_ref, s0_ref,
                wxs_ref, wbc_ref, bxs_ref, bbc_ref, vec_ref, ng_ref, *rest,
                t, length, heads, groups):
    mixed_ref, s_out, conv_out, xs_ext, bc_ext, s_scr, ybuf = rest
    c = pl.program_id(1)
    c_last = (length - 1) // t
    v_last = length - c_last * t
    inner = xs_ref.shape[1]
    p = inner // heads
    n = s_scr.shape[2]
    hpg = heads // groups
    gw = inner // groups
    cdt = BF16 if t % 16 == 0 else F32

    @pl.when(c == 0)
    def _():
        xs_ext[pl.ds(EXT0 - HIST, HIST), :] = xconv0_ref[0]
        bc_ext[pl.ds(EXT0 - HIST, HIST), :] = bconv0_ref[0]
        s_scr[...] = s0_ref[0]

    rowmask = _iota2((t, 1), 0) < (length - c * t)
    xs = _silu(_causal_conv(xs_ext, jnp.where(rowmask, xs_ref[...], 0.0), wxs_ref[...], t) + bxs_ref[...])
    bc = _silu(_causal_conv(bc_ext, jnp.where(rowmask, bc_ref[...], 0.0), wbc_ref[...], t) + bbc_ref[...])

    @pl.when(c == c_last)
    def _():
        conv_out[0, :, :inner] = xs_ext[pl.ds(EXT0 + v_last - HIST, HIST), :]
        conv_out[0, :, inner:] = bc_ext[pl.ds(EXT0 + v_last - HIST, HIST), :]

    xs_ext[pl.ds(EXT0 - HIST, HIST), :] = xs_ext[pl.ds(EXT0 + t - HIST, HIST), :]
    bc_ext[pl.ds(EXT0 - HIST, HIST), :] = bc_ext[pl.ds(EXT0 + t - HIST, HIST), :]

    dt = jnp.where(rowmask, jax.nn.softplus(jnp.where(rowmask, dt_ref[...], 0.0) + vec_ref[0:1, :]), 0.0)
    da = dt * (-jnp.exp(vec_ref[1:2, :]))
    d_skip = vec_ref[2:3, :]
    cs, cst = _cumsum_rows(da)
    ecs = jnp.exp(cs)
    e_last = jnp.exp(cs[t - 1:t, :])
    w_dec = jnp.exp(cs[t - 1:t, :] - cs)
    causal = _iota2((t, t), 0) >= _iota2((t, t), 1)
    gn = groups * n
    lane_g = _iota2((t, gw), 1)
    row_g = _iota2((gw, 1), 0)
    for g in range(groups):
        e0 = g * hpg
        bm = bc[:, g * n:(g + 1) * n].astype(cdt)
        cm = bc[:, gn + g * n:gn + (g + 1) * n].astype(cdt)
        cb = _dot_nt(cm, bm)
        xg = xs[:, g * gw:(g + 1) * gw]
        xdt = xg * _expand_heads(dt, e0, hpg, p)
        m_cat = jnp.concatenate(
            [(cb * jnp.where(causal, jnp.exp(cs[:, e0 + j:e0 + j + 1] - cst[e0 + j:e0 + j + 1, :]), 0.0)).astype(cdt)
             for j in range(hpg)], axis=1)
        x_bd = jnp.concatenate(
            [jnp.where((lane_g >= j * p) & (lane_g < (j + 1) * p), xdt, 0.0) for j in range(hpg)], axis=0).astype(cdt)
        sg = s_scr[pl.ds(e0, hpg)].reshape(gw, n)
        y = _dot(m_cat, x_bd) + _dot_nt(cm, sg.astype(cdt)) * _expand_heads(ecs, e0, hpg, p)
        el = e_last[:, e0:e0 + 1]
        for j in range(1, hpg):
            el = jnp.where(row_g >= j * p, e_last[:, e0 + j:e0 + j + 1], el)
        s_new = sg * el + _dot_tn((xdt * _expand_heads(w_dec, e0, hpg, p)).astype(cdt), bm)
        s_scr[pl.ds(e0, hpg)] = s_new.reshape(hpg, p, n)
        ybuf[:, g * gw:(g + 1) * gw] = y + xg * _expand_heads(d_skip, e0, hpg, p)
    yz = ybuf[...] * _silu(z_ref[...])
    ng = ng_ref[...]
    for g in range(groups):
        seg = yz[:, g * gw:(g + 1) * gw]
        seg = seg * lax.rsqrt(jnp.mean(seg * seg, axis=-1, keepdims=True) + RMS_EPS) * ng[:, g * gw:(g + 1) * gw]
        mixed_ref[:, g * gw:(g + 1) * gw] = jnp.where(rowmask, seg, 0.0)

    @pl.when(c == c_last)
    def _():
        s_out[0] = s_scr[...]


def _ssd_mixer(proj, conv0, s0, wts, *, t, length, batch, cpb, rb0, groups):
    cw, cb, vec, ng = wts
    heads = s0.shape[1]
    inner = heads * s0.shape[2]
    rmap = lambda j: (lambda b, c: (rb0 + b * cpb + c, j))
    bmap4 = lambda b, c: (b, 0, 0, 0)
    in_specs = [
        pl.BlockSpec((t, inner), rmap(0)), pl.BlockSpec((t, inner), rmap(1)), pl.BlockSpec((t, inner), rmap(2)),
        pl.BlockSpec((t, LANE), rmap(3 * inner // LANE)),
        pl.BlockSpec((1, HIST, inner), lambda b, c: (b, 0, 0)),
        pl.BlockSpec((1, HIST, inner), lambda b, c: (b, 0, 1)),
        pl.BlockSpec((1,) + s0.shape[1:], bmap4),
        pl.BlockSpec((CONV_W, inner), lambda b, c: (0, 0)), pl.BlockSpec((CONV_W, inner), lambda b, c: (0, 1)),
        pl.BlockSpec((1, inner), lambda b, c: (0, 0)), pl.BlockSpec((1, inner), lambda b, c: (0, 1)),
        pl.BlockSpec(vec.shape, lambda b, c: (0, 0)), pl.BlockSpec(ng.shape, lambda b, c: (0, 0)),
    ]
    args = [proj, proj, proj, proj, conv0, conv0, s0, cw, cw, cb, cb, vec, ng]
    out_shape = (jax.ShapeDtypeStruct((batch * cpb * t, inner), F32),
                 jax.ShapeDtypeStruct(s0.shape, F32),
                 jax.ShapeDtypeStruct(conv0.shape, F32))
    out_specs = (pl.BlockSpec((t, inner), lambda b, c: (b * cpb + c, 0)),
                 pl.BlockSpec((1,) + s0.shape[1:], bmap4),
                 pl.BlockSpec((1, HIST, conv0.shape[2]), lambda b, c: (b, 0, 0)))
    return pl.pallas_call(
        functools.partial(_ssd_kernel, t=t, length=length, heads=heads, groups=groups),
        grid=(batch, cpb),
        in_specs=in_specs, out_specs=out_specs, out_shape=out_shape,
        scratch_shapes=[pltpu.VMEM((t + EXT0, inner), F32), pltpu.VMEM((t + EXT0, inner), F32),
                        pltpu.VMEM(s0.shape[1:], F32), pltpu.VMEM((t, inner), F32)],
        compiler_params=pltpu.CompilerParams(dimension_semantics=("parallel", "arbitrary"),
                                             vmem_limit_bytes=VMEM_LIMIT),
        name="ssd_mixer")(*args)


def _layer_norm(h, g, b):
    mu = jnp.mean(h, axis=-1, keepdims=True)
    d = h - mu
    var = jnp.mean(d * d, axis=-1, keepdims=True)
    return d * lax.rsqrt(var + LN_EPS) * g + b


def _col_to_rows(v):
    eye = _iota2((LANE, LANE), 0) == _iota2((LANE, LANE), 1)
    rows = [jnp.sum(jnp.where(eye, v[i * LANE:(i + 1) * LANE, :], 0.0), axis=0, keepdims=True)
            for i in range(v.shape[0] // LANE)]
    return jnp.concatenate(rows, axis=0)


def _outln_kernel(yp_ref, ys_ref, x_ref, w_ref, ln_ref, wr_ref, br_ref, o_ref, bkt_ref, rank_ref, cnt_ref,
                  ybuf, cnt_scr, *, alpha, n_groups, epg, nbp, n_buckets):
    d = x_ref.shape[1]
    i = pl.program_id(0)

    @pl.when(i == 0)
    def _():
        cnt_scr[...] = jnp.zeros_like(cnt_scr)

    @pl.when(i < nbp)
    def _():
        ybuf[...] = yp_ref[...].astype(BF16)

    @pl.when(i >= nbp)
    def _():
        ybuf[...] = ys_ref[...].astype(BF16)

    acc = _dot(ybuf[...], w_ref[...])
    xn = _layer_norm(alpha * x_ref[...] + acc, ln_ref[0:1, :], ln_ref[1:2, :])
    o_ref[:, :d] = xn
    logits = _dot(xn, wr_ref[...], precision=HI) + br_ref[...]
    tm = logits.shape[0]
    lane = _iota2((tm, ROUTE_LANES), 1)
    ninf = -jnp.inf
    gl = jnp.where(lane < n_groups, logits, ninf)
    gmax = jnp.max(gl, axis=-1, keepdims=True)
    gidx = jnp.min(jnp.where(gl == gmax, lane, ROUTE_LANES), axis=-1, keepdims=True)
    g_w = 1.0 / jnp.sum(jnp.where(lane < n_groups, jnp.exp(logits - gmax), 0.0), axis=-1, keepdims=True)
    lo = n_groups + epg * gidx
    el = jnp.where((lane >= lo) & (lane < lo + epg), logits, ninf)
    m1 = jnp.max(el, axis=-1, keepdims=True)
    i1 = jnp.min(jnp.where(el == m1, lane, ROUTE_LANES), axis=-1, keepdims=True)
    el2 = jnp.where(lane == i1, ninf, el)
    m2 = jnp.max(el2, axis=-1, keepdims=True)
    i2 = jnp.min(jnp.where(el2 == m2, lane, ROUTE_LANES), axis=-1, keepdims=True)
    r = jnp.exp(m2 - m1)
    w1 = g_w / (1.0 + r)
    w2 = g_w * r / (1.0 + r)
    e1 = i1 - lo
    e2 = i2 - lo
    first = e1 < e2
    ea = jnp.where(first, e1, e2)
    eb = jnp.where(first, e2, e1)
    bucket = gidx * PAIR_SLOTS + ea * epg + eb
    ga = jnp.where(first, w1, w2)
    gb = jnp.where(first, w2, w1)
    o_ref[:, d:] = jnp.where(lane == 1, ga, jnp.where(lane == 2, gb, 0.0))
    onehot = _iota2((tm, n_buckets), 1) == bucket
    oh = jnp.where(onehot, 1.0, 0.0)
    earlier = jnp.where(_iota2((tm, tm), 0) > _iota2((tm, tm), 1), 1.0, 0.0).astype(BF16)
    before = _dot(earlier, oh.astype(BF16)) + cnt_scr[...]
    rank = jnp.sum(jnp.where(onehot, before, 0.0), axis=-1, keepdims=True)
    cnt_scr[...] = cnt_scr[...] + jnp.sum(oh, axis=0, keepdims=True)
    bkt_ref[0] = _col_to_rows(bucket.astype(F32)).astype(jnp.int32)
    rank_ref[0] = _col_to_rows(rank).astype(jnp.int32)
    cnt_ref[...] = jnp.broadcast_to(cnt_scr[...], cnt_ref.shape).astype(jnp.int32)


def _outln(mixed_p, mixed_s, x, w, ln, wr, br, *, n, tm, alpha, n_groups, epg, n_buckets):
    d = x.shape[1]
    dm = mixed_p.shape[1]
    nbp = mixed_p.shape[0] // tm
    nb = n // tm
    return pl.pallas_call(
        functools.partial(_outln_kernel, alpha=alpha, n_groups=n_groups, epg=epg, nbp=nbp, n_buckets=n_buckets),
        grid=(nb,),
        in_specs=[pl.BlockSpec((tm, dm), lambda i: (jnp.minimum(i, nbp - 1), 0)),
                  pl.BlockSpec((tm, dm), lambda i: (jnp.maximum(i - nbp, 0), 0)),
                  pl.BlockSpec((tm, d), lambda i: (i, 0)),
                  pl.BlockSpec((dm, d), lambda i: (0, 0), pipeline_mode=pl.Buffered(1)),
                  pl.BlockSpec(ln.shape, lambda i: (0, 0)),
                  pl.BlockSpec(wr.shape, lambda i: (0, 0)),
                  pl.BlockSpec(br.shape, lambda i: (0, 0))],
        out_specs=(pl.BlockSpec((tm, d + ROUTE_LANES), lambda i: (i, 0)),
                   pl.BlockSpec((1, tm // LANE, LANE), lambda i: (i, 0, 0)),
                   pl.BlockSpec((1, tm // LANE, LANE), lambda i: (i, 0, 0)),
                   pl.BlockSpec((8, n_buckets), lambda i: (0, 0))),
        out_shape=(jax.ShapeDtypeStruct((n, d + ROUTE_LANES), F32),
                   jax.ShapeDtypeStruct((nb, tm // LANE, LANE), jnp.int32),
                   jax.ShapeDtypeStruct((nb, tm // LANE, LANE), jnp.int32),
                   jax.ShapeDtypeStruct((8, n_buckets), jnp.int32)),
        scratch_shapes=[pltpu.VMEM((tm, dm), BF16), pltpu.VMEM((1, n_buckets), F32)],
        compiler_params=pltpu.CompilerParams(dimension_semantics=("arbitrary",), vmem_limit_bytes=VMEM_LIMIT),
        name="outproj_ln_route")(mixed_p, mixed_s, x, w, ln, wr, br)


def _moe_kernel(valid_ref, ea_ref, eb_ref, src_ref, nxt_ref, x_hbm, w1a_ref, w3a_ref, w2a_ref, w1b_ref, w3b_ref,
                w2b_ref, ln_ref, o_hbm, xbuf, obuf, gsem, ssem, *, alpha, tr, n_rows, unroll):
    t = pl.program_id(0)
    nt = pl.num_programs(0)
    d = obuf.shape[2]
    slot = lax.rem(t, 2)
    other = 1 - slot
    v_t = valid_ref[t] > 0
    v_next = jnp.logical_and(t + 1 < nt, valid_ref[jnp.minimum(t + 1, nt - 1)] > 0)

    def gather(idx_ref, s):
        def body(i, carry):
            for u in range(unroll):
                r = i * unroll + u
                pltpu.make_async_copy(x_hbm.at[pl.ds(jnp.minimum(idx_ref[0, 0, r], n_rows - 1), 1), :],
                                      xbuf.at[s, pl.ds(r, 1), :], gsem.at[s]).start()
            return carry
        lax.fori_loop(0, tr // unroll, body, 0)

    def wait_gather(s):
        pltpu.make_async_copy(x_hbm.at[pl.ds(0, tr), :], xbuf.at[s], gsem.at[s]).wait()

    def scatter(s):
        def body(i, carry):
            for u in range(unroll):
                r = i * unroll + u
                pltpu.make_async_copy(obuf.at[s, pl.ds(r, 1), :],
                                      o_hbm.at[pl.ds(src_ref[0, 0, r], 1), :], ssem.at[s]).start()
            return carry
        lax.fori_loop(0, tr // unroll, body, 0)

    def wait_scatter(s):
        pltpu.make_async_copy(obuf.at[s], o_hbm.at[pl.ds(0, tr), :], ssem.at[s]).wait()

    @pl.when(t == 0)
    def _():
        obuf[...] = jnp.zeros_like(obuf)
        for s in range(2):
            fill = pltpu.make_async_copy(obuf.at[s], o_hbm.at[pl.ds(n_rows + s * tr, tr), :], ssem.at[s])
            fill.start()
            fill.wait()

    @pl.when(jnp.logical_and(t == 0, v_t))
    def _():
        gather(src_ref, 0)

    @pl.when(v_next)
    def _():
        gather(nxt_ref, other)

    @pl.when(v_t)
    def _():
        wait_gather(slot)
        x = xbuf[slot, :, :d]
        ga = xbuf[slot, :, d + 1:d + 2]
        gb = xbuf[slot, :, d + 2:d + 3]
        xb = x.astype(BF16)
        ha = _silu(_dot(xb, w1a_ref[0])) * _dot(xb, w3a_ref[0])
        hb = _silu(_dot(xb, w1b_ref[0])) * _dot(xb, w3b_ref[0])
        ffn = _dot((ha * ga).astype(BF16), w2a_ref[0]) + _dot((hb * gb).astype(BF16), w2b_ref[0])
        y = _layer_norm(alpha * x + ffn, ln_ref[0:1, :], ln_ref[1:2, :])

        @pl.when(t >= 2)
        def _():
            wait_scatter(slot)

        obuf[slot] = y
        scatter(slot)

        @pl.when(jnp.logical_not(v_next))
        def _():
            wait_scatter(slot)

            @pl.when(t >= 1)
            def _():
                wait_scatter(other)


def _moe(x1r, src, valid, ea, eb, w1, w3, w2, ln, *, alpha, tr):
    n, dr = x1r.shape
    d = dr - ROUTE_LANES
    nt = valid.shape[0]
    f = w1.shape[2]
    wa = lambda t, v, a, b: (a[t], 0, 0)
    wb = lambda t, v, a, b: (b[t], 0, 0)
    grid_spec = pltpu.PrefetchScalarGridSpec(
        num_scalar_prefetch=3,
        grid=(nt,),
        in_specs=[pl.BlockSpec((1, 1, tr), lambda t, v, a, b: (t, 0, 0), memory_space=pltpu.SMEM),
                  pl.BlockSpec((1, 1, tr), lambda t, v, a, b: (jnp.minimum(t + 1, nt - 1), 0, 0),
                               memory_space=pltpu.SMEM),
                  pl.BlockSpec(memory_space=pl.ANY),
                  pl.BlockSpec((1, d, f), wa), pl.BlockSpec((1, d, f), wa), pl.BlockSpec((1, f, d), wa),
                  pl.BlockSpec((1, d, f), wb), pl.BlockSpec((1, d, f), wb), pl.BlockSpec((1, f, d), wb),
                  pl.BlockSpec(ln.shape, lambda t, v, a, b: (0, 0))],
        out_specs=pl.BlockSpec(memory_space=pl.ANY),
        scratch_shapes=[pltpu.VMEM((2, tr, dr), F32), pltpu.VMEM((2, tr, d), F32),
                        pltpu.SemaphoreType.DMA((2,)), pltpu.SemaphoreType.DMA((2,))])
    return pl.pallas_call(
        functools.partial(_moe_kernel, alpha=alpha, tr=tr, n_rows=n, unroll=8),
        grid_spec=grid_spec,
        out_shape=jax.ShapeDtypeStruct((n + 2 * tr, d), F32),
        compiler_params=pltpu.CompilerParams(dimension_semantics=("arbitrary",), vmem_limit_bytes=VMEM_LIMIT),
        name="moe_routed")(valid, ea, eb, src, src, x1r, w1, w3, w2, w1, w3, w2, ln)


def _route_schedule(bucket, rank, counts, *, n, tr, nt, epg, e_off):
    nbk = counts.shape[0]
    tiles_b = (counts + tr - 1) // tr
    tile_end = jnp.cumsum(tiles_b)
    pstart = (tile_end - tiles_b) * tr
    hit = bucket[:, None] == jnp.arange(nbk, dtype=jnp.int32)[None, :]
    dest = jnp.sum(jnp.where(hit, pstart[None, :], 0), axis=1) + rank
    slots = jnp.arange(nt * tr, dtype=jnp.int32)
    trash = n + ((slots // tr) % 2) * tr + slots % tr
    src = trash.at[dest].set(jnp.arange(n, dtype=jnp.int32))
    tiles = jnp.arange(nt, dtype=jnp.int32)
    valid = tiles < tile_end[-1]
    tb = jnp.sum(tile_end[None, :] <= jnp.minimum(tiles, tile_end[-1] - 1)[:, None], axis=1).astype(jnp.int32)
    grp = tb // PAIR_SLOTS
    pair = tb % PAIR_SLOTS
    ea = e_off + grp * epg + pair // epg
    eb = e_off + grp * epg + pair % epg
    return src.reshape(nt, 1, tr), valid.astype(jnp.int32), ea.astype(jnp.int32), eb.astype(jnp.int32)


def _pad_cols(w, width):
    return jnp.pad(w, ((0, 0), (0, width - w.shape[1])))


def _pad_vec(v, width=LANE):
    return jnp.pad(v.astype(F32), (0, width - v.shape[0]))[None, :]


def _rows8(*vecs):
    m = jnp.stack([v.astype(F32) for v in vecs], axis=0)
    return jnp.pad(m, ((0, 8 - m.shape[0]), (0, 0)))


def _block_diag(w):
    nb, bd, _ = w.shape
    eye = jnp.eye(nb, dtype=w.dtype)
    return jnp.einsum("nde,nm->ndme", w, eye).reshape(nb * bd, nb * bd)


def kernel(x_prompt, x_sample, state_delta, state_delta_conv, state_lru, state_lru_conv, state_ssm, state_ssm_conv,
           meta, w_in_ab, dn_conv_w, dn_a_log, dn_dt_bias, dn_norm_g, lru_conv_w, lru_conv_b, lru_w_a, lru_b_a,
           lru_w_x, lru_b_x, lru_lam, w_out_ab, w_in_ssd, ssd_conv_w, ssd_conv_b, ssd_dt_bias, ssd_a_log, ssd_d,
           ssd_norm_g, w_out_ssd, ln_g, ln_b, moe_w_group, moe_b_group, moe_w_expert, moe_b_expert, moe_w1, moe_w3,
           moe_w2):
    bp, seq, d = x_prompt.shape
    bs, ls, _ = x_sample.shape
    depth = ln_g.shape[0]
    alpha = (2.0 * depth) ** 0.25
    n_groups, epg = moe_w1.shape[1], moe_w1.shape[2]
    heads_dn = dn_a_log.shape[1]
    nqk = heads_dn * state_delta.shape[3]
    nv = heads_dn * state_delta.shape[4]
    lru_w = lru_lam.shape[1]
    heads_ssd = ssd_a_log.shape[1]
    inner = heads_ssd * state_ssm.shape[3]
    n_state = state_ssm.shape[4]
    ssd_groups = (ssd_conv_w.shape[2] - inner) // (2 * n_state)
    assert nqk == nv and state_delta.shape[3] == LANE and state_delta.shape[4] == LANE and n_state == LANE

    lp_len = N_META + seq
    chunk_lcm = math.lcm(DN_CHUNK, SSD_CHUNK)
    lpad = -(-lp_len // chunk_lcm) * chunk_lcm
    np_rows = bp * lpad
    ns_rows = bs * ls
    ntot = np_rows + ns_rows
    tm = math.gcd(math.gcd(np_rows, ns_rows), ROW_TILE)
    assert tm % LANE == 0 and np_rows % ls == 0 and ls % 8 == 0 and ls >= HIST
    assert (lp_len - 1) % DN_CHUNK + 1 >= HIST and (lp_len - 1) % SSD_CHUNK + 1 >= HIST

    xp = jnp.concatenate([jnp.broadcast_to(meta.astype(F32), (bp, N_META, d)), x_prompt,
                          jnp.zeros((bp, lpad - lp_len, d), F32)], axis=1)
    x = jnp.concatenate([xp.reshape(np_rows, d), x_sample.reshape(ns_rows, d)], axis=0)

    tr = MOE_TILE
    n_buckets = n_groups * PAIR_SLOTS
    n_pairs = n_groups * (epg * (epg - 1) // 2)
    nt = -(-(ntot + n_pairs * (tr - 1)) // tr)
    n_exp = n_groups * epg
    w1 = moe_w1.reshape(depth * n_exp, d, -1).astype(BF16)
    w3 = moe_w3.reshape(depth * n_exp, d, -1).astype(BF16)
    w2 = moe_w2.reshape(depth * n_exp, -1, d).astype(BF16)

    zeros = lambda *s: jnp.zeros(s, F32)
    outs = {k: [] for k in ("pd", "pdc", "pl", "plc", "ps", "psc", "sd", "sdc", "sl", "slc", "ss", "ssc")}
    for layer in range(depth):
        i = layer // 2
        if layer % 2 == 0:
            wi = w_in_ab[i]
            o_b = 3 * nqk + nv
            w_in = jnp.concatenate([wi[:, :o_b], wi[:, o_b + 2 * heads_dn:],
                                    _pad_cols(wi[:, o_b:o_b + heads_dn], LANE),
                                    _pad_cols(wi[:, o_b + heads_dn:o_b + 2 * heads_dn], LANE)], axis=1).astype(BF16)
            proj = _inproj(x, w_in, tm, ntot)
            wts = (dn_conv_w[i], _rows8(_pad_vec(dn_a_log[i])[0], _pad_vec(dn_dt_bias[i])[0], dn_norm_g[i]),
                   lru_conv_w[i], _rows8(lru_conv_b[i], lru_b_a[i], lru_b_x[i], lru_lam[i]),
                   _block_diag(lru_w_a[i]).astype(BF16), _block_diag(lru_w_x[i]).astype(BF16))
            mixed_p, s_p, dc_p, h_p, lc_p = _ab_mixer(
                proj, zeros(bp, HIST, 3 * nqk), zeros(bp, *state_delta.shape[2:]), zeros(bp, HIST, lru_w),
                zeros(bp, 1, lru_w), wts, t=DN_CHUNK, length=lp_len, batch=bp, cpb=lpad // DN_CHUNK, rb0=0)
            mixed_s, s_s, dc_s, h_s, lc_s = _ab_mixer(
                proj, state_delta_conv[i], state_delta[i], state_lru_conv[i], state_lru[i][:, None, :], wts,
                t=ls, length=ls, batch=bs, cpb=1, rb0=np_rows // ls)
            outs["pd"].append(s_p); outs["pdc"].append(dc_p); outs["pl"].append(h_p[:, 0]); outs["plc"].append(lc_p)
            outs["sd"].append(s_s); outs["sdc"].append(dc_s); outs["sl"].append(h_s[:, 0]); outs["slc"].append(lc_s)
            w_out = w_out_ab[i].astype(BF16)
        else:
            wi = w_in_ssd[i]
            e_raw = wi.shape[1]
            w_in = _pad_cols(wi, -(-e_raw // LANE) * LANE).astype(BF16)
            proj = _inproj(x, w_in, tm, ntot)
            wts = (ssd_conv_w[i], ssd_conv_b[i][None, :],
                   _rows8(_pad_vec(ssd_dt_bias[i])[0], _pad_vec(ssd_a_log[i])[0], _pad_vec(ssd_d[i])[0]),
                   ssd_norm_g[i][None, :])
            mixed_p, s_p, c_p = _ssd_mixer(
                proj, zeros(bp, HIST, ssd_conv_w.shape[2]), zeros(bp, *state_ssm.shape[2:]), wts,
                t=SSD_CHUNK, length=lp_len, batch=bp, cpb=lpad // SSD_CHUNK, rb0=0, groups=ssd_groups)
            mixed_s, s_s, c_s = _ssd_mixer(
                proj, state_ssm_conv[i], state_ssm[i], wts,
                t=ls, length=ls, batch=bs, cpb=1, rb0=np_rows // ls, groups=ssd_groups)
            outs["ps"].append(s_p); outs["psc"].append(c_p)
            outs["ss"].append(s_s); outs["ssc"].append(c_s)
            w_out = w_out_ssd[i].astype(BF16)
        wr = _pad_cols(jnp.concatenate([moe_w_group[layer], moe_w_expert[layer]], axis=1), ROUTE_LANES)
        br = _pad_vec(jnp.concatenate([moe_b_group[layer], moe_b_expert[layer]]), ROUTE_LANES)
        x1r, bucket, rank, counts = _outln(mixed_p, mixed_s, x, w_out, _rows8(ln_g[layer, 0], ln_b[layer, 0]), wr, br,
                                           n=ntot, tm=tm, alpha=alpha, n_groups=n_groups, epg=epg,
                                           n_buckets=n_buckets)
        src, valid, ea, eb = _route_schedule(bucket.reshape(ntot), rank.reshape(ntot), counts[0], n=ntot, tr=tr,
                                             nt=nt, epg=epg, e_off=layer * n_exp)
        x = _moe(x1r, src, valid, ea, eb, w1, w3, w2, _rows8(ln_g[layer, 1], ln_b[layer, 1]), alpha=alpha, tr=tr)

    y_prompt = x[:np_rows].reshape(bp, lpad, d)[:, N_META:lp_len]
    y_sample = x[np_rows:ntot].reshape(bs, ls, d)
    st = lambda k: jnp.stack(outs[k], axis=0)
    return (y_prompt, y_sample, st("pd"), st("pdc"), st("pl"), st("plc"), st("ps"), st("psc"),
            st("sd"), st("sdc"), st("sl"), st("slc"), st("ss"), st("ssc"))
```

```python
import functools
import math

import jax
import jax.numpy as jnp
from jax import lax
from jax.experimental import pallas as pl
from jax.experimental.pallas import tpu as pltpu

F32 = jnp.float32
BF16 = jnp.bfloat16
HI = lax.Precision.HIGHEST

LN_EPS = 1e-5
RMS_EPS = 1e-6
LRU_C = 8.0
N_META = 16
CONV_W = 4
HIST = CONV_W - 1

LANE = 128
EXT0 = 8
DN_CHUNK = 64
SSD_CHUNK = 128
ROW_TILE = 512
MOE_TILE = 128
DMA_ROWS = 8
PAIR_SLOTS = 64
ROUTE_LANES = 128
VMEM_LIMIT = 56 * 1024 * 1024


def _silu(x):
    return x * jax.nn.sigmoid(x)


def _dot(a, b, **kw):
    return jnp.dot(a, b, preferred_element_type=F32, **kw)


def _dot_nt(a, b, **kw):
    return lax.dot_general(a, b, (((1,), (1,)), ((), ())), preferred_element_type=F32, **kw)


def _dot_tn(a, b, **kw):
    return lax.dot_general(a, b, (((0,), (0,)), ((), ())), preferred_element_type=F32, **kw)


def _iota2(shape, dim):
    return lax.broadcasted_iota(jnp.int32, shape, dim)


def _cumsum_rows(x):
    t = x.shape[0]
    rows = _iota2((t, 1), 0)
    cs = x
    sh = 1
    while sh < t:
        cs = cs + jnp.where(rows >= sh, pltpu.roll(cs, sh, 0), 0.0)
        sh *= 2
    if t == LANE:
        cst = cs.T
    else:
        eye = (_iota2((LANE, LANE), 0) == _iota2((LANE, LANE), 1)).astype(F32)
        cst = _dot_nt(eye, cs, precision=HI)
    return cs, cst


def _expand_heads(v, e0, nh, p):
    lane = _iota2((v.shape[0], nh * p), 1)
    out = jnp.broadcast_to(v[:, e0:e0 + 1], (v.shape[0], nh * p))
    for j in range(1, nh):
        out = jnp.where(lane >= j * p, v[:, e0 + j:e0 + j + 1], out)
    return out


def _causal_conv(ext_ref, x, w, t):
    ext_ref[pl.ds(EXT0, t), :] = x
    y = ext_ref[pl.ds(EXT0 - HIST, t), :] * w[0:1, :]
    for k in range(1, CONV_W):
        y = y + ext_ref[pl.ds(EXT0 - HIST + k, t), :] * w[k:k + 1, :]
    return y


def _inproj_kernel(x_ref, w_ref, o_ref, *, tn):
    xb = x_ref[...].astype(BF16)
    for j in range(o_ref.shape[1] // tn):
        o_ref[:, j * tn:(j + 1) * tn] = _dot(xb, w_ref[:, j * tn:(j + 1) * tn])


def _inproj(x, w, tm, n):
    k = x.shape[1]
    e = w.shape[1]
    tn = max(c for c in range(LANE, 1024 + LANE, LANE) if e % c == 0)
    return pl.pallas_call(
        functools.partial(_inproj_kernel, tn=tn),
        grid=(n // tm,),
        in_specs=[pl.BlockSpec((tm, k), lambda i: (i, 0)),
                  pl.BlockSpec((k, e), lambda i: (0, 0), pipeline_mode=pl.Buffered(1))],
        out_specs=pl.BlockSpec((tm, e), lambda i: (i, 0)),
        out_shape=jax.ShapeDtypeStruct((n, e), F32),
        compiler_params=pltpu.CompilerParams(dimension_semantics=("parallel",), vmem_limit_bytes=VMEM_LIMIT),
        name="inproj")(x, w)


def _ab_kernel(qkv_ref, z_ref, xb_ref, yb_ref, bg_ref, dconv0_ref, s0_ref, lconv0_ref, h0_ref,
               dnw_ref, dnv_ref, lw_ref, lv_ref, wa_ref, wx_ref, *rest,
               t, length, heads):
    mixed_ref, s_out, dconv_out, h_out, lconv_out, qkv_ext, xb_ext, s_scr, h_scr = rest
    c = pl.program_id(1)
    c_last = (length - 1) // t
    v_last = length - c_last * t
    dk = LANE
    nqk = heads * dk
    cdt = BF16 if t % 16 == 0 else F32

    @pl.when(c == 0)
    def _():
        qkv_ext[pl.ds(EXT0 - HIST, HIST), :] = dconv0_ref[0]
        xb_ext[pl.ds(EXT0 - HIST, HIST), :] = lconv0_ref[0]
        s_scr[...] = s0_ref[0]
        h_scr[...] = h0_ref[0]

    rowmask = _iota2((t, 1), 0) < (length - c * t)

    x = jnp.where(rowmask, qkv_ref[...], 0.0)
    qkv = _silu(_causal_conv(qkv_ext, x, dnw_ref[...], t))

    @pl.when(c == c_last)
    def _():
        dconv_out[0] = qkv_ext[pl.ds(EXT0 + v_last - HIST, HIST), :]

    qkv_ext[pl.ds(EXT0 - HIST, HIST), :] = qkv_ext[pl.ds(EXT0 + t - HIST, HIST), :]

    bg = jnp.where(rowmask, bg_ref[...], 0.0)
    a_log = dnv_ref[0:1, :]
    dt_bias = dnv_ref[1:2, :]
    norm_g = dnv_ref[2:3, :]
    beta = jnp.where(rowmask, jax.nn.sigmoid(bg[:, :LANE]), 0.0)
    g = jnp.where(rowmask, -jnp.exp(a_log) * jax.nn.softplus(bg[:, LANE:] + dt_bias), 0.0)
    rows = _iota2((t, 1), 0)
    gc = g
    sh = 1
    while sh < t:
        gc = gc + jnp.where(rows >= sh, pltpu.roll(gc, sh, 0), 0.0)
        sh *= 2
    r = heads * t
    stack = lambda f: jnp.concatenate([f(h) for h in range(heads)], axis=0)
    l2n = lambda a: a * lax.rsqrt(jnp.sum(a * a, axis=-1, keepdims=True) + RMS_EPS)
    q = stack(lambda h: l2n(qkv[:, h * dk:(h + 1) * dk]) * (dk ** -0.5))
    k = stack(lambda h: l2n(qkv[:, nqk + h * dk:nqk + (h + 1) * dk]))
    v = stack(lambda h: qkv[:, 2 * nqk + h * dk:2 * nqk + (h + 1) * dk])
    beta_c = stack(lambda h: beta[:, h:h + 1])
    gc_c = stack(lambda h: gc[:, h:h + 1])
    row = _iota2((r, r), 0)
    col = _iota2((r, r), 1)
    gc_r = jnp.sum(jnp.where(row == col, gc_c, 0.0), axis=0, keepdims=True)
    same = (row // t) == (col // t)
    decay = jnp.where(same & (row >= col), jnp.exp(gc_c - gc_r), 0.0)
    eg = jnp.exp(gc_c)
    kb = k * beta_c
    kc = k.astype(cdt)
    a_mat = jnp.where(same & (row > col), _dot_nt(kb.astype(cdt), kc) * decay, 0.0)
    y = jnp.concatenate([v * beta_c, kb * eg], axis=-1)
    p = -a_mat
    nsteps = max(1, (t - 1).bit_length())
    for i in range(nsteps):
        pc = p.astype(cdt)
        y = y + _dot(pc, y.astype(cdt))
        if i + 1 < nsteps:
            p = _dot(pc, pc)
    u = y[:, :dk]
    w = y[:, dk:]
    qe = q * eg
    ws_qs = [_dot(jnp.concatenate([w[h * t:(h + 1) * t], qe[h * t:(h + 1) * t]], axis=0).astype(cdt),
                  s_scr[h].astype(cdt)) for h in range(heads)]
    v_new = u - jnp.concatenate([a[:t] for a in ws_qs], axis=0)
    attn = _dot_nt(q.astype(cdt), kc) * decay
    o = jnp.concatenate([a[t:] for a in ws_qs], axis=0) + _dot(attn.astype(cdt), v_new.astype(cdt))
    o = o * lax.rsqrt(jnp.mean(o * o, axis=-1, keepdims=True) + RMS_EPS) * norm_g
    z = z_ref[...]
    for h in range(heads):
        g_last = gc[t - 1:t, h:h + 1]
        k_dec = (k[h * t:(h + 1) * t] * jnp.exp(g_last - gc[:, h:h + 1])).astype(cdt)
        s_scr[h] = s_scr[h] * jnp.exp(g_last) + _dot_tn(k_dec, v_new[h * t:(h + 1) * t].astype(cdt))
        mixed_ref[:, h * dk:(h + 1) * dk] = jnp.where(rowmask, o[h * t:(h + 1) * t] * _silu(z[:, h * dk:(h + 1) * dk]), 0.0)

    xb = jnp.where(rowmask, xb_ref[...], 0.0)
    xc = _causal_conv(xb_ext, xb, lw_ref[...], t) + lv_ref[0:1, :]

    @pl.when(c == c_last)
    def _():
        lconv_out[0] = xb_ext[pl.ds(EXT0 + v_last - HIST, HIST), :]

    xb_ext[pl.ds(EXT0 - HIST, HIST), :] = xb_ext[pl.ds(EXT0 + t - HIST, HIST), :]
    xcb = xc.astype(BF16)
    r = jax.nn.sigmoid(_dot(xcb, wa_ref[...]) + lv_ref[1:2, :])
    gi = jax.nn.sigmoid(_dot(xcb, wx_ref[...]) + lv_ref[2:3, :])
    log_a = jnp.where(rowmask, -LRU_C * r * jax.nn.softplus(-lv_ref[3:4, :]), 0.0)
    a = jnp.exp(log_a)
    uu = jnp.where(rowmask, jnp.sqrt(1.0 - a * a) * (gi * xc), 0.0)
    rows = _iota2((t, 1), 0)
    sh = 1
    while sh < t:
        keep = rows >= sh
        uu = jnp.where(keep, a * pltpu.roll(uu, sh, 0) + uu, uu)
        a = jnp.where(keep, a * pltpu.roll(a, sh, 0), a)
        sh *= 2
    hh = uu + a * h_scr[...]
    h_scr[...] = hh[t - 1:t, :]
    yb = yb_ref[...]
    gelu = 0.5 * yb * (1.0 + jnp.tanh(math.sqrt(2.0 / math.pi) * (yb + 0.044715 * (yb * yb * yb))))
    mixed_ref[:, nqk:] = jnp.where(rowmask, gelu * hh, 0.0)

    @pl.when(c == c_last)
    def _():
        s_out[0] = s_scr[...]
        h_out[0] = h_scr[...]


def _ab_mixer(proj, dconv0, s0, lconv0, h0, wts, *, t, length, batch, cpb, rb0):
    dnw, dnv, lw, lv, wa, wx = wts
    heads = s0.shape[1]
    nqk = heads * LANE
    nv = heads * s0.shape[3]
    lw_ch = lw.shape[1]
    dmix = nv + lw_ch
    rmap = lambda j: (lambda b, c: (rb0 + b * cpb + c, j))
    bmap3 = lambda b, c: (b, 0, 0)
    bmap4 = lambda b, c: (b, 0, 0, 0)
    full2 = lambda b, c: (0, 0)
    in_specs = [
        pl.BlockSpec((t, 3 * nqk), rmap(0)),
        pl.BlockSpec((t, nv), rmap(3 * nqk // nv)),
        pl.BlockSpec((t, lw_ch), rmap((3 * nqk + nv) // lw_ch)),
        pl.BlockSpec((t, lw_ch), rmap((3 * nqk + nv) // lw_ch + 1)),
        pl.BlockSpec((t, 2 * LANE), rmap((3 * nqk + nv + 2 * lw_ch) // (2 * LANE))),
        pl.BlockSpec((1, HIST, 3 * nqk), bmap3),
        pl.BlockSpec((1,) + s0.shape[1:], bmap4),
        pl.BlockSpec((1, HIST, lw_ch), bmap3),
        pl.BlockSpec((1, 1, lw_ch), bmap3),
        pl.BlockSpec(dnw.shape, full2), pl.BlockSpec(dnv.shape, full2),
        pl.BlockSpec(lw.shape, full2), pl.BlockSpec(lv.shape, full2),
        pl.BlockSpec(wa.shape, full2), pl.BlockSpec(wx.shape, full2),
    ]
    args = [proj, proj, proj, proj, proj, dconv0, s0, lconv0, h0, dnw, dnv, lw, lv, wa, wx]
    out_shape = (jax.ShapeDtypeStruct((batch * cpb * t, dmix), F32),
                 jax.ShapeDtypeStruct(s0.shape, F32),
                 jax.ShapeDtypeStruct(dconv0.shape, F32),
                 jax.ShapeDtypeStruct(h0.shape, F32),
                 jax.ShapeDtypeStruct(lconv0.shape, F32))
    out_specs = (pl.BlockSpec((t, dmix), lambda b, c: (b * cpb + c, 0)),
                 pl.BlockSpec((1,) + s0.shape[1:], bmap4),
                 pl.BlockSpec((1, HIST, 3 * nqk), bmap3),
                 pl.BlockSpec((1, 1, lw_ch), bmap3),
                 pl.BlockSpec((1, HIST, lw_ch), bmap3))
    return pl.pallas_call(
        functools.partial(_ab_kernel, t=t, length=length, heads=heads),
        grid=(batch, cpb),
        in_specs=in_specs, out_specs=out_specs, out_shape=out_shape,
        scratch_shapes=[pltpu.VMEM((t + EXT0, 3 * nqk), F32), pltpu.VMEM((t + EXT0, lw_ch), F32),
                        pltpu.VMEM(s0.shape[1:], F32), pltpu.VMEM((1, lw_ch), F32)],
        compiler_params=pltpu.CompilerParams(dimension_semantics=("parallel", "arbitrary"),
                                             vmem_limit_bytes=VMEM_LIMIT),
        name="ab_mixer")(*args)


def _ssd_kernel(z_ref, xs_ref, bc_ref, dt_ref, xconv0_ref, bconv0_ref, s0_ref,
                wxs_ref, wbc_ref, bxs_ref, bbc_ref, vec_ref, ng_ref, *rest,
                t, length, heads, groups):
    mixed_ref, s_out, conv_out, xs_ext, bc_ext, s_scr, ybuf = rest
    c = pl.program_id(1)
    c_last = (length - 1) // t
    v_last = length - c_last * t
    inner = xs_ref.shape[1]
    p = inner // heads
    n = s_scr.shape[2]
    hpg = heads // groups
    gw = inner // groups
    cdt = BF16 if t % 16 == 0 else F32

    @pl.when(c == 0)
    def _():
        xs_ext[pl.ds(EXT0 - HIST, HIST), :] = xconv0_ref[0]
        bc_ext[pl.ds(EXT0 - HIST, HIST), :] = bconv0_ref[0]
        s_scr[...] = s0_ref[0]

    rowmask = _iota2((t, 1), 0) < (length - c * t)
    xs = _silu(_causal_conv(xs_ext, jnp.where(rowmask, xs_ref[...], 0.0), wxs_ref[...], t) + bxs_ref[...])
    bc = _silu(_causal_conv(bc_ext, jnp.where(rowmask, bc_ref[...], 0.0), wbc_ref[...], t) + bbc_ref[...])

    @pl.when(c == c_last)
    def _():
        conv_out[0, :, :inner] = xs_ext[pl.ds(EXT0 + v_last - HIST, HIST), :]
        conv_out[0, :, inner:] = bc_ext[pl.ds(EXT0 + v_last - HIST, HIST), :]

    xs_ext[pl.ds(EXT0 - HIST, HIST), :] = xs_ext[pl.ds(EXT0 + t - HIST, HIST), :]
    bc_ext[pl.ds(EXT0 - HIST, HIST), :] = bc_ext[pl.ds(EXT0 + t - HIST, HIST), :]

    dt = jnp.where(rowmask, jax.nn.softplus(jnp.where(rowmask, dt_ref[...], 0.0) + vec_ref[0:1, :]), 0.0)
    da = dt * (-jnp.exp(vec_ref[1:2, :]))
    d_skip = vec_ref[2:3, :]
    cs, cst = _cumsum_rows(da)
    ecs = jnp.exp(cs)
    e_last = jnp.exp(cs[t - 1:t, :])
    w_dec = jnp.exp(cs[t - 1:t, :] - cs)
    causal = _iota2((t, t), 0) >= _iota2((t, t), 1)
    gn = groups * n
    lane_g = _iota2((t, gw), 1)
    row_g = _iota2((gw, 1), 0)
    for g in range(groups):
        e0 = g * hpg
        bm = bc[:, g * n:(g + 1) * n].astype(cdt)
        cm = bc[:, gn + g * n:gn + (g + 1) * n].astype(cdt)
        cb = _dot_nt(cm, bm)
        xg = xs[:, g * gw:(g + 1) * gw]
        xdt = xg * _expand_heads(dt, e0, hpg, p)
        m_cat = jnp.concatenate(
            [(cb * jnp.where(causal, jnp.exp(cs[:, e0 + j:e0 + j + 1] - cst[e0 + j:e0 + j + 1, :]), 0.0)).astype(cdt)
             for j in range(hpg)], axis=1)
        x_bd = jnp.concatenate(
            [jnp.where((lane_g >= j * p) & (lane_g < (j + 1) * p), xdt, 0.0) for j in range(hpg)], axis=0).astype(cdt)
        sg = s_scr[pl.ds(e0, hpg)].reshape(gw, n)
        y = _dot(m_cat, x_bd) + _dot_nt(cm, sg.astype(cdt)) * _expand_heads(ecs, e0, hpg, p)
        el = e_last[:, e0:e0 + 1]
        for j in range(1, hpg):
            el = jnp.where(row_g >= j * p, e_last[:, e0 + j:e0 + j + 1], el)
        s_new = sg * el + _dot_tn((xdt * _expand_heads(w_dec, e0, hpg, p)).astype(cdt), bm)
        s_scr[pl.ds(e0, hpg)] = s_new.reshape(hpg, p, n)
        ybuf[:, g * gw:(g + 1) * gw] = y + xg * _expand_heads(d_skip, e0, hpg, p)
    yz = ybuf[...] * _silu(z_ref[...])
    ng = ng_ref[...]
    for g in range(groups):
        seg = yz[:, g * gw:(g + 1) * gw]
        seg = seg * lax.rsqrt(jnp.mean(seg * seg, axis=-1, keepdims=True) + RMS_EPS) * ng[:, g * gw:(g + 1) * gw]
        mixed_ref[:, g * gw:(g + 1) * gw] = jnp.where(rowmask, seg, 0.0)

    @pl.when(c == c_last)
    def _():
        s_out[0] = s_scr[...]


def _ssd_mixer(proj, conv0, s0, wts, *, t, length, batch, cpb, rb0, groups):
    cw, cb, vec, ng = wts
    heads = s0.shape[1]
    inner = heads * s0.shape[2]
    rmap = lambda j: (lambda b, c: (rb0 + b * cpb + c, j))
    bmap4 = lambda b, c: (b, 0, 0, 0)
    in_specs = [
        pl.BlockSpec((t, inner), rmap(0)), pl.BlockSpec((t, inner), rmap(1)), pl.BlockSpec((t, inner), rmap(2)),
        pl.BlockSpec((t, LANE), rmap(3 * inner // LANE)),
        pl.BlockSpec((1, HIST, inner), lambda b, c: (b, 0, 0)),
        pl.BlockSpec((1, HIST, inner), lambda b, c: (b, 0, 1)),
        pl.BlockSpec((1,) + s0.shape[1:], bmap4),
        pl.BlockSpec((CONV_W, inner), lambda b, c: (0, 0)), pl.BlockSpec((CONV_W, inner), lambda b, c: (0, 1)),
        pl.BlockSpec((1, inner), lambda b, c: (0, 0)), pl.BlockSpec((1, inner), lambda b, c: (0, 1)),
        pl.BlockSpec(vec.shape, lambda b, c: (0, 0)), pl.BlockSpec(ng.shape, lambda b, c: (0, 0)),
    ]
    args = [proj, proj, proj, proj, conv0, conv0, s0, cw, cw, cb, cb, vec, ng]
    out_shape = (jax.ShapeDtypeStruct((batch * cpb * t, inner), F32),
                 jax.ShapeDtypeStruct(s0.shape, F32),
                 jax.ShapeDtypeStruct(conv0.shape, F32))
    out_specs = (pl.BlockSpec((t, inner), lambda b, c: (b * cpb + c, 0)),
                 pl.BlockSpec((1,) + s0.shape[1:], bmap4),
                 pl.BlockSpec((1, HIST, conv0.shape[2]), lambda b, c: (b, 0, 0)))
    return pl.pallas_call(
        functools.partial(_ssd_kernel, t=t, length=length, heads=heads, groups=groups),
        grid=(batch, cpb),
        in_specs=in_specs, out_specs=out_specs, out_shape=out_shape,
        scratch_shapes=[pltpu.VMEM((t + EXT0, inner), F32), pltpu.VMEM((t + EXT0, inner), F32),
                        pltpu.VMEM(s0.shape[1:], F32), pltpu.VMEM((t, inner), F32)],
        compiler_params=pltpu.CompilerParams(dimension_semantics=("parallel", "arbitrary"),
                                             vmem_limit_bytes=VMEM_LIMIT),
        name="ssd_mixer")(*args)


def _layer_norm(h, g, b):
    mu = jnp.mean(h, axis=-1, keepdims=True)
    d = h - mu
    var = jnp.mean(d * d, axis=-1, keepdims=True)
    return d * lax.rsqrt(var + LN_EPS) * g + b


def _col_to_rows(v):
    eye = _iota2((LANE, LANE), 0) == _iota2((LANE, LANE), 1)
    rows = [jnp.sum(jnp.where(eye, v[i * LANE:(i + 1) * LANE, :], 0.0), axis=0, keepdims=True)
            for i in range(v.shape[0] // LANE)]
    return jnp.concatenate(rows, axis=0)


def _outln_kernel(yp_ref, ys_ref, x_ref, w_ref, ln_ref, wr_ref, br_ref, o_ref, bkt_ref, rank_ref, cnt_ref,
                  ybuf, cnt_scr, *, alpha, n_groups, epg, nbp, n_buckets):
    d = x_ref.shape[1]
    i = pl.program_id(0)

    @pl.when(i == 0)
    def _():
        cnt_scr[...] = jnp.zeros_like(cnt_scr)

    @pl.when(i < nbp)
    def _():
        ybuf[...] = yp_ref[...].astype(BF16)

    @pl.when(i >= nbp)
    def _():
        ybuf[...] = ys_ref[...].astype(BF16)

    acc = _dot(ybuf[...], w_ref[...])
    xn = _layer_norm(alpha * x_ref[...] + acc, ln_ref[0:1, :], ln_ref[1:2, :])
    o_ref[:, :d] = xn
    xh = xn.astype(BF16)
    xl = (xn - xh.astype(F32)).astype(BF16)
    hw = _dot(xh, wr_ref[...])
    logits = hw[:, :ROUTE_LANES] + hw[:, ROUTE_LANES:] + _dot(xl, wr_ref[:, :ROUTE_LANES]) + br_ref[...]
    tm = logits.shape[0]
    lane = _iota2((tm, ROUTE_LANES), 1)
    ninf = -jnp.inf
    gl = jnp.where(lane < n_groups, logits, ninf)
    gmax = jnp.max(gl, axis=-1, keepdims=True)
    gidx = jnp.min(jnp.where(gl == gmax, lane, ROUTE_LANES), axis=-1, keepdims=True)
    g_w = 1.0 / jnp.sum(jnp.where(lane < n_groups, jnp.exp(logits - gmax), 0.0), axis=-1, keepdims=True)
    lo = n_groups + epg * gidx
    el = jnp.where((lane >= lo) & (lane < lo + epg), logits, ninf)
    m1 = jnp.max(el, axis=-1, keepdims=True)
    i1 = jnp.min(jnp.where(el == m1, lane, ROUTE_LANES), axis=-1, keepdims=True)
    el2 = jnp.where(lane == i1, ninf, el)
    m2 = jnp.max(el2, axis=-1, keepdims=True)
    i2 = jnp.min(jnp.where(el2 == m2, lane, ROUTE_LANES), axis=-1, keepdims=True)
    r = jnp.exp(m2 - m1)
    w1 = g_w / (1.0 + r)
    w2 = g_w * r / (1.0 + r)
    e1 = i1 - lo
    e2 = i2 - lo
    first = e1 < e2
    ea = jnp.where(first, e1, e2)
    eb = jnp.where(first, e2, e1)
    bucket = gidx * PAIR_SLOTS + ea * epg + eb
    ga = jnp.where(first, w1, w2)
    gb = jnp.where(first, w2, w1)
    o_ref[:, d:] = jnp.where(lane == 1, ga, jnp.where(lane == 2, gb, 0.0))
    onehot = _iota2((tm, n_buckets), 1) == bucket
    oh = jnp.where(onehot, 1.0, 0.0)
    earlier = jnp.where(_iota2((tm, tm), 0) > _iota2((tm, tm), 1), 1.0, 0.0).astype(BF16)
    before = _dot(earlier, oh.astype(BF16)) + cnt_scr[...]
    rank = jnp.sum(jnp.where(onehot, before, 0.0), axis=-1, keepdims=True)
    cnt_scr[...] = cnt_scr[...] + jnp.sum(oh, axis=0, keepdims=True)
    bkt_ref[0] = _col_to_rows(bucket.astype(F32)).astype(jnp.int32)
    rank_ref[0] = _col_to_rows(rank).astype(jnp.int32)
    cnt_ref[...] = jnp.broadcast_to(cnt_scr[...], cnt_ref.shape).astype(jnp.int32)


def _outln(mixed_p, mixed_s, x, w, ln, wr, br, *, n, tm, alpha, n_groups, epg, n_buckets):
    d = x.shape[1]
    dm = mixed_p.shape[1]
    nbp = mixed_p.shape[0] // tm
    nb = n // tm
    return pl.pallas_call(
        functools.partial(_outln_kernel, alpha=alpha, n_groups=n_groups, epg=epg, nbp=nbp, n_buckets=n_buckets),
        grid=(nb,),
        in_specs=[pl.BlockSpec((tm, dm), lambda i: (jnp.minimum(i, nbp - 1), 0)),
                  pl.BlockSpec((tm, dm), lambda i: (jnp.maximum(i - nbp, 0), 0)),
                  pl.BlockSpec((tm, d), lambda i: (i, 0)),
                  pl.BlockSpec((dm, d), lambda i: (0, 0), pipeline_mode=pl.Buffered(1)),
                  pl.BlockSpec(ln.shape, lambda i: (0, 0)),
                  pl.BlockSpec(wr.shape, lambda i: (0, 0)),
                  pl.BlockSpec(br.shape, lambda i: (0, 0))],
        out_specs=(pl.BlockSpec((tm, d + ROUTE_LANES), lambda i: (i, 0)),
                   pl.BlockSpec((1, tm // LANE, LANE), lambda i: (i, 0, 0)),
                   pl.BlockSpec((1, tm // LANE, LANE), lambda i: (i, 0, 0)),
                   pl.BlockSpec((8, n_buckets), lambda i: (0, 0))),
        out_shape=(jax.ShapeDtypeStruct((n, d + ROUTE_LANES), F32),
                   jax.ShapeDtypeStruct((nb, tm // LANE, LANE), jnp.int32),
                   jax.ShapeDtypeStruct((nb, tm // LANE, LANE), jnp.int32),
                   jax.ShapeDtypeStruct((8, n_buckets), jnp.int32)),
        scratch_shapes=[pltpu.VMEM((tm, dm), BF16), pltpu.VMEM((1, n_buckets), F32)],
        compiler_params=pltpu.CompilerParams(dimension_semantics=("arbitrary",), vmem_limit_bytes=VMEM_LIMIT),
        name="outproj_ln_route")(mixed_p, mixed_s, x, w, ln, wr, br)


def _moe_kernel(c8_ref, ea_ref, eb_ref, src_ref, nxt_ref, x_hbm, w1a_ref, w3a_ref, w2a_ref, w1b_ref, w3b_ref,
                w2b_ref, ln_ref, o_hbm, xbuf, obuf, gsem, ssem, *, alpha, tr, n_rows):
    t = pl.program_id(0)
    nt = pl.num_programs(0)
    d = obuf.shape[2]
    slot = lax.rem(t, 2)
    other = 1 - slot
    c8_t = c8_ref[t]
    c8_next = jnp.where(t + 1 < nt, c8_ref[jnp.minimum(t + 1, nt - 1)], 0)

    def gather(idx_ref, s, c8):
        for g in range(tr // DMA_ROWS):
            @pl.when(g < c8)
            def _():
                for u in range(DMA_ROWS):
                    r = g * DMA_ROWS + u
                    pltpu.make_async_copy(x_hbm.at[pl.ds(jnp.minimum(idx_ref[0, 0, r], n_rows - 1), 1), :],
                                          xbuf.at[s, pl.ds(r, 1), :], gsem.at[s]).start(priority=u % 2)

    def scatter(s, c8):
        for g in range(tr // DMA_ROWS):
            @pl.when(g < c8)
            def _():
                for u in range(DMA_ROWS):
                    r = g * DMA_ROWS + u
                    pltpu.make_async_copy(obuf.at[s, pl.ds(r, 1), :],
                                          o_hbm.at[pl.ds(src_ref[0, 0, r], 1), :], ssem.at[s]).start(priority=u % 2)

    def wait_groups(desc, c8):
        def body(i, carry):
            desc.wait()
            return carry
        lax.fori_loop(0, c8, body, 0)

    def wait_gather(s, c8):
        wait_groups(pltpu.make_async_copy(x_hbm.at[pl.ds(0, DMA_ROWS), :], xbuf.at[s, pl.ds(0, DMA_ROWS), :],
                                          gsem.at[s]), c8)

    def wait_scatter(s, c8):
        wait_groups(pltpu.make_async_copy(obuf.at[s, pl.ds(0, DMA_ROWS), :], o_hbm.at[pl.ds(0, DMA_ROWS), :],
                                          ssem.at[s]), c8)

    @pl.when(t == 0)
    def _():
        xbuf[...] = jnp.zeros_like(xbuf)
        obuf[...] = jnp.zeros_like(obuf)
        for s in range(2):
            fill = pltpu.make_async_copy(obuf.at[s], o_hbm.at[pl.ds(n_rows + s * tr, tr), :], ssem.at[s])
            fill.start()
            fill.wait()
        gather(src_ref, 0, c8_t)

    gather(nxt_ref, other, c8_next)

    @pl.when(c8_t > 0)
    def _():
        wait_gather(slot, c8_t)
        x = xbuf[slot, :, :d]
        ga = xbuf[slot, :, d + 1:d + 2]
        gb = xbuf[slot, :, d + 2:d + 3]
        xb = x.astype(BF16)
        ha = _silu(_dot(xb, w1a_ref[0])) * _dot(xb, w3a_ref[0])
        hb = _silu(_dot(xb, w1b_ref[0])) * _dot(xb, w3b_ref[0])
        ffn = _dot((ha * ga).astype(BF16), w2a_ref[0]) + _dot((hb * gb).astype(BF16), w2b_ref[0])
        y = _layer_norm(alpha * x + ffn, ln_ref[0:1, :], ln_ref[1:2, :])

        @pl.when(t >= 2)
        def _():
            wait_scatter(slot, c8_ref[jnp.maximum(t - 2, 0)])

        obuf[slot] = y
        scatter(slot, c8_t)

        @pl.when(c8_next == 0)
        def _():
            wait_scatter(slot, c8_t)

            @pl.when(t >= 1)
            def _():
                wait_scatter(other, c8_ref[jnp.maximum(t - 1, 0)])


def _moe(x1r, src, c8, ea, eb, w1, w3, w2, ln, *, alpha, tr):
    n, dr = x1r.shape
    d = dr - ROUTE_LANES
    nt = c8.shape[0]
    f = w1.shape[2]
    wa = lambda t, v, a, b: (a[t], 0, 0)
    wb = lambda t, v, a, b: (b[t], 0, 0)
    grid_spec = pltpu.PrefetchScalarGridSpec(
        num_scalar_prefetch=3,
        grid=(nt,),
        in_specs=[pl.BlockSpec((1, 1, tr), lambda t, v, a, b: (t, 0, 0), memory_space=pltpu.SMEM),
                  pl.BlockSpec((1, 1, tr), lambda t, v, a, b: (jnp.minimum(t + 1, nt - 1), 0, 0),
                               memory_space=pltpu.SMEM),
                  pl.BlockSpec(memory_space=pl.ANY),
                  pl.BlockSpec((1, d, f), wa), pl.BlockSpec((1, d, f), wa), pl.BlockSpec((1, f, d), wa),
                  pl.BlockSpec((1, d, f), wb), pl.BlockSpec((1, d, f), wb), pl.BlockSpec((1, f, d), wb),
                  pl.BlockSpec(ln.shape, lambda t, v, a, b: (0, 0))],
        out_specs=pl.BlockSpec(memory_space=pl.ANY),
        scratch_shapes=[pltpu.VMEM((2, tr, dr), F32), pltpu.VMEM((2, tr, d), F32),
                        pltpu.SemaphoreType.DMA((2,)), pltpu.SemaphoreType.DMA((2,))])
    return pl.pallas_call(
        functools.partial(_moe_kernel, alpha=alpha, tr=tr, n_rows=n),
        grid_spec=grid_spec,
        out_shape=jax.ShapeDtypeStruct((n + 2 * tr, d), F32),
        compiler_params=pltpu.CompilerParams(dimension_semantics=("arbitrary",), vmem_limit_bytes=VMEM_LIMIT),
        name="moe_routed")(c8, ea, eb, src, src, x1r, w1, w3, w2, w1, w3, w2, ln)


def _route_schedule(bucket, rank, counts, *, n, tr, nt, epg, e_off):
    nbk = counts.shape[0]
    tiles_b = (counts + tr - 1) // tr
    tile_end = jnp.cumsum(tiles_b)
    pstart = (tile_end - tiles_b) * tr
    hit = bucket[:, None] == jnp.arange(nbk, dtype=jnp.int32)[None, :]
    dest = jnp.sum(jnp.where(hit, pstart[None, :], 0), axis=1) + rank
    slots = jnp.arange(nt * tr, dtype=jnp.int32)
    trash = n + ((slots // tr) % 2) * tr + slots % tr
    src = trash.at[dest].set(jnp.arange(n, dtype=jnp.int32))
    tiles = jnp.arange(nt, dtype=jnp.int32)
    tb = jnp.sum(tile_end[None, :] <= jnp.minimum(tiles, tile_end[-1] - 1)[:, None], axis=1).astype(jnp.int32)
    own = tb[:, None] == jnp.arange(nbk, dtype=jnp.int32)[None, :]
    pick = lambda tab: jnp.sum(jnp.where(own, tab[None, :], 0), axis=1)
    rows_t = jnp.clip(pick(counts) - (tiles - pick(tile_end - tiles_b)) * tr, 0, tr)
    c8 = jnp.where(tiles < tile_end[-1], (rows_t + DMA_ROWS - 1) // DMA_ROWS, 0)
    grp = tb // PAIR_SLOTS
    pair = tb % PAIR_SLOTS
    ea = e_off + grp * epg + pair // epg
    eb = e_off + grp * epg + pair % epg
    return src.reshape(nt, 1, tr), c8.astype(jnp.int32), ea.astype(jnp.int32), eb.astype(jnp.int32)


def _pad_cols(w, width):
    return jnp.pad(w, ((0, 0), (0, width - w.shape[1])))


def _pad_vec(v, width=LANE):
    return jnp.pad(v.astype(F32), (0, width - v.shape[0]))[None, :]


def _rows8(*vecs):
    m = jnp.stack([v.astype(F32) for v in vecs], axis=0)
    return jnp.pad(m, ((0, 8 - m.shape[0]), (0, 0)))


def _block_diag(w):
    nb, bd, _ = w.shape
    eye = jnp.eye(nb, dtype=w.dtype)
    return jnp.einsum("nde,nm->ndme", w, eye).reshape(nb * bd, nb * bd)


def kernel(x_prompt, x_sample, state_delta, state_delta_conv, state_lru, state_lru_conv, state_ssm, state_ssm_conv,
           meta, w_in_ab, dn_conv_w, dn_a_log, dn_dt_bias, dn_norm_g, lru_conv_w, lru_conv_b, lru_w_a, lru_b_a,
           lru_w_x, lru_b_x, lru_lam, w_out_ab, w_in_ssd, ssd_conv_w, ssd_conv_b, ssd_dt_bias, ssd_a_log, ssd_d,
           ssd_norm_g, w_out_ssd, ln_g, ln_b, moe_w_group, moe_b_group, moe_w_expert, moe_b_expert, moe_w1, moe_w3,
           moe_w2):
    bp, seq, d = x_prompt.shape
    bs, ls, _ = x_sample.shape
    depth = ln_g.shape[0]
    alpha = (2.0 * depth) ** 0.25
    n_groups, epg = moe_w1.shape[1], moe_w1.shape[2]
    heads_dn = dn_a_log.shape[1]
    nqk = heads_dn * state_delta.shape[3]
    nv = heads_dn * state_delta.shape[4]
    lru_w = lru_lam.shape[1]
    heads_ssd = ssd_a_log.shape[1]
    inner = heads_ssd * state_ssm.shape[3]
    n_state = state_ssm.shape[4]
    ssd_groups = (ssd_conv_w.shape[2] - inner) // (2 * n_state)
    assert nqk == nv and state_delta.shape[3] == LANE and state_delta.shape[4] == LANE and n_state == LANE

    lp_len = N_META + seq
    chunk_lcm = math.lcm(DN_CHUNK, SSD_CHUNK)
    lpad = -(-lp_len // chunk_lcm) * chunk_lcm
    np_rows = bp * lpad
    ns_rows = bs * ls
    ntot = np_rows + ns_rows
    tm = math.gcd(math.gcd(np_rows, ns_rows), ROW_TILE)
    assert tm % LANE == 0 and np_rows % ls == 0 and ls % 8 == 0 and ls >= HIST
    assert (lp_len - 1) % DN_CHUNK + 1 >= HIST and (lp_len - 1) % SSD_CHUNK + 1 >= HIST

    xp = jnp.concatenate([jnp.broadcast_to(meta.astype(F32), (bp, N_META, d)), x_prompt,
                          jnp.zeros((bp, lpad - lp_len, d), F32)], axis=1)
    x = jnp.concatenate([xp.reshape(np_rows, d), x_sample.reshape(ns_rows, d)], axis=0)

    tr = MOE_TILE
    n_buckets = n_groups * PAIR_SLOTS
    n_pairs = n_groups * (epg * (epg - 1) // 2)
    nt = -(-(ntot + n_pairs * (tr - 1)) // tr)
    n_exp = n_groups * epg
    w1 = moe_w1.reshape(depth * n_exp, d, -1).astype(BF16)
    w3 = moe_w3.reshape(depth * n_exp, d, -1).astype(BF16)
    w2 = moe_w2.reshape(depth * n_exp, -1, d).astype(BF16)

    zeros = lambda *s: jnp.zeros(s, F32)
    outs = {k: [] for k in ("pd", "pdc", "pl", "plc", "ps", "psc", "sd", "sdc", "sl", "slc", "ss", "ssc")}
    for layer in range(depth):
        i = layer // 2
        if layer % 2 == 0:
            wi = w_in_ab[i]
            o_b = 3 * nqk + nv
            w_in = jnp.concatenate([wi[:, :o_b], wi[:, o_b + 2 * heads_dn:],
                                    _pad_cols(wi[:, o_b:o_b + heads_dn], LANE),
                                    _pad_cols(wi[:, o_b + heads_dn:o_b + 2 * heads_dn], LANE)], axis=1).astype(BF16)
            proj = _inproj(x, w_in, tm, ntot)
            wts = (dn_conv_w[i], _rows8(_pad_vec(dn_a_log[i])[0], _pad_vec(dn_dt_bias[i])[0], dn_norm_g[i]),
                   lru_conv_w[i], _rows8(lru_conv_b[i], lru_b_a[i], lru_b_x[i], lru_lam[i]),
                   _block_diag(lru_w_a[i]).astype(BF16), _block_diag(lru_w_x[i]).astype(BF16))
            mixed_p, s_p, dc_p, h_p, lc_p = _ab_mixer(
                proj, zeros(bp, HIST, 3 * nqk), zeros(bp, *state_delta.shape[2:]), zeros(bp, HIST, lru_w),
                zeros(bp, 1, lru_w), wts, t=DN_CHUNK, length=lp_len, batch=bp, cpb=lpad // DN_CHUNK, rb0=0)
            mixed_s, s_s, dc_s, h_s, lc_s = _ab_mixer(
                proj, state_delta_conv[i], state_delta[i], state_lru_conv[i], state_lru[i][:, None, :], wts,
                t=ls, length=ls, batch=bs, cpb=1, rb0=np_rows // ls)
            outs["pd"].append(s_p); outs["pdc"].append(dc_p); outs["pl"].append(h_p[:, 0]); outs["plc"].append(lc_p)
            outs["sd"].append(s_s); outs["sdc"].append(dc_s); outs["sl"].append(h_s[:, 0]); outs["slc"].append(lc_s)
            w_out = w_out_ab[i].astype(BF16)
        else:
            wi = w_in_ssd[i]
            e_raw = wi.shape[1]
            w_in = _pad_cols(wi, -(-e_raw // LANE) * LANE).astype(BF16)
            proj = _inproj(x, w_in, tm, ntot)
            wts = (ssd_conv_w[i], ssd_conv_b[i][None, :],
                   _rows8(_pad_vec(ssd_dt_bias[i])[0], _pad_vec(ssd_a_log[i])[0], _pad_vec(ssd_d[i])[0]),
                   ssd_norm_g[i][None, :])
            mixed_p, s_p, c_p = _ssd_mixer(
                proj, zeros(bp, HIST, ssd_conv_w.shape[2]), zeros(bp, *state_ssm.shape[2:]), wts,
                t=SSD_CHUNK, length=lp_len, batch=bp, cpb=lpad // SSD_CHUNK, rb0=0, groups=ssd_groups)
            mixed_s, s_s, c_s = _ssd_mixer(
                proj, state_ssm_conv[i], state_ssm[i], wts,
                t=ls, length=ls, batch=bs, cpb=1, rb0=np_rows // ls, groups=ssd_groups)
            outs["ps"].append(s_p); outs["psc"].append(c_p)
            outs["ss"].append(s_s); outs["ssc"].append(c_s)
            w_out = w_out_ssd[i].astype(BF16)
        wr = _pad_cols(jnp.concatenate([moe_w_group[layer], moe_w_expert[layer]], axis=1), ROUTE_LANES)
        wr_hi = wr.astype(BF16)
        wr = jnp.concatenate([wr_hi, (wr - wr_hi.astype(F32)).astype(BF16)], axis=1)
        br = _pad_vec(jnp.concatenate([moe_b_group[layer], moe_b_expert[layer]]), ROUTE_LANES)
        x1r, bucket, rank, counts = _outln(mixed_p, mixed_s, x, w_out, _rows8(ln_g[layer, 0], ln_b[layer, 0]), wr, br,
                                           n=ntot, tm=tm, alpha=alpha, n_groups=n_groups, epg=epg,
                                           n_buckets=n_buckets)
        src, c8, ea, eb = _route_schedule(bucket.reshape(ntot), rank.reshape(ntot), counts[0], n=ntot, tr=tr,
                                             nt=nt, epg=epg, e_off=layer * n_exp)
        x = _moe(x1r, src, c8, ea, eb, w1, w3, w2, _rows8(ln_g[layer, 1], ln_b[layer, 1]), alpha=alpha, tr=tr)

    y_prompt = x[:np_rows].reshape(bp, lpad, d)[:, N_META:lp_len]
    y_sample = x[np_rows:ntot].reshape(bs, ls, d)
    st = lambda k: jnp.stack(outs[k], axis=0)
    return (y_prompt, y_sample, st("pd"), st("pdc"), st("pl"), st("plc"), st("ps"), st("psc"),
            st("sd"), st("sdc"), st("sl"), st("slc"), st("ss"), st("ssc"))
```

```python
import functools
import math

import jax
import jax.numpy as jnp
from jax import lax
from jax.experimental import pallas as pl
from jax.experimental.pallas import tpu as pltpu

F32 = jnp.float32
BF16 = jnp.bfloat16
HI = lax.Precision.HIGHEST

LN_EPS = 1e-5
RMS_EPS = 1e-6
LRU_C = 8.0
N_META = 16
CONV_W = 4
HIST = CONV_W - 1

LANE = 128
EXT0 = 8
DN_CHUNK = 64
SSD_CHUNK = 128
ROW_TILE = 512
MOE_TILE = 128
DMA_ROWS = 8
PAIR_SLOTS = 64
ROUTE_LANES = 128
VMEM_LIMIT = 56 * 1024 * 1024


def _silu(x):
    return x * jax.nn.sigmoid(x)


def _dot(a, b, **kw):
    return jnp.dot(a, b, preferred_element_type=F32, **kw)


def _dot_nt(a, b, **kw):
    return lax.dot_general(a, b, (((1,), (1,)), ((), ())), preferred_element_type=F32, **kw)


def _dot_tn(a, b, **kw):
    return lax.dot_general(a, b, (((0,), (0,)), ((), ())), preferred_element_type=F32, **kw)


def _iota2(shape, dim):
    return lax.broadcasted_iota(jnp.int32, shape, dim)


def _cumsum_rows(x):
    t = x.shape[0]
    rows = _iota2((t, 1), 0)
    cs = x
    sh = 1
    while sh < t:
        cs = cs + jnp.where(rows >= sh, pltpu.roll(cs, sh, 0), 0.0)
        sh *= 2
    if t == LANE:
        cst = cs.T
    else:
        eye = (_iota2((LANE, LANE), 0) == _iota2((LANE, LANE), 1)).astype(F32)
        cst = _dot_nt(eye, cs, precision=HI)
    return cs, cst


def _expand_heads(v, e0, nh, p):
    lane = _iota2((v.shape[0], nh * p), 1)
    out = jnp.broadcast_to(v[:, e0:e0 + 1], (v.shape[0], nh * p))
    for j in range(1, nh):
        out = jnp.where(lane >= j * p, v[:, e0 + j:e0 + j + 1], out)
    return out


def _put_state(ref, ow, val, sl=()):
    ref[(ow, 0) + sl] = val
    for j in range(ref.shape[0]):
        if j != ow:
            ref[(j, 0) + sl] = jnp.zeros_like(val)


def _causal_conv(ext_ref, x, w, t):
    ext_ref[pl.ds(EXT0, t), :] = x
    ext = ext_ref[...]
    y = x * w[HIST:CONV_W, :]
    for k in range(HIST):
        y = y + pltpu.roll(ext, HIST - k, 0)[EXT0:EXT0 + t, :] * w[k:k + 1, :]
    return y


def _inproj_kernel(x_ref, w_ref, o_ref, *, tn):
    xb = x_ref[...].astype(BF16)
    for j in range(o_ref.shape[1] // tn):
        o_ref[:, j * tn:(j + 1) * tn] = _dot(xb, w_ref[:, j * tn:(j + 1) * tn])


def _inproj(x, w, tm, n):
    k = x.shape[1]
    e = w.shape[1]
    tn = max(c for c in range(LANE, 1024 + LANE, LANE) if e % c == 0)
    return pl.pallas_call(
        functools.partial(_inproj_kernel, tn=tn),
        grid=(n // tm,),
        in_specs=[pl.BlockSpec((tm, k), lambda i: (i, 0)),
                  pl.BlockSpec((k, e), lambda i: (0, 0), pipeline_mode=pl.Buffered(1))],
        out_specs=pl.BlockSpec((tm, e), lambda i: (i, 0)),
        out_shape=jax.ShapeDtypeStruct((n, e), F32),
        compiler_params=pltpu.CompilerParams(dimension_semantics=("parallel",), vmem_limit_bytes=VMEM_LIMIT),
        name="inproj")(x, w)


def _ab_kernel(qkv_ref, z_ref, xb_ref, yb_ref, bg_ref, dconv0_ref, s0_ref, lconv0_ref, h0_ref,
               dnw_ref, dnv_ref, lw_ref, lv_ref, wa_ref, wx_ref, *rest,
               t, length, heads, ow):
    mixed_ref, s_out, dconv_out, h_out, lconv_out, qkv_ext, xb_ext, s_scr, h_scr = rest[-9:]
    c = pl.program_id(1)
    c_last = (length - 1) // t
    v_last = length - c_last * t
    dk = LANE
    nqk = heads * dk
    cdt = BF16 if t % 16 == 0 else F32

    @pl.when(c == 0)
    def _():
        qkv_ext[pl.ds(0, EXT0), :] = jnp.zeros((EXT0, qkv_ext.shape[1]), F32)
        xb_ext[pl.ds(0, EXT0), :] = jnp.zeros((EXT0, xb_ext.shape[1]), F32)
        qkv_ext[pl.ds(EXT0 - HIST, HIST), :] = dconv0_ref[0, 0]
        xb_ext[pl.ds(EXT0 - HIST, HIST), :] = lconv0_ref[0, 0]
        s_scr[...] = s0_ref[0, 0]
        h_scr[...] = h0_ref[0]

    rowmask = _iota2((t, 1), 0) < (length - c * t)

    x = jnp.where(rowmask, qkv_ref[...], 0.0)
    qkv = _silu(_causal_conv(qkv_ext, x, dnw_ref[...], t))

    @pl.when(c == c_last)
    def _():
        _put_state(dconv_out, ow, qkv_ext[pl.ds(EXT0 + v_last - HIST, HIST), :])

    qkv_ext[pl.ds(EXT0 - HIST, HIST), :] = qkv_ext[pl.ds(EXT0 + t - HIST, HIST), :]

    bg = jnp.where(rowmask, bg_ref[...], 0.0)
    a_log = dnv_ref[0:1, :]
    dt_bias = dnv_ref[1:2, :]
    norm_g = dnv_ref[2:3, :]
    beta = jnp.where(rowmask, jax.nn.sigmoid(bg[:, :LANE]), 0.0)
    g = jnp.where(rowmask, -jnp.exp(a_log) * jax.nn.softplus(bg[:, LANE:] + dt_bias), 0.0)
    rows = _iota2((t, 1), 0)
    gc = g
    sh = 1
    while sh < t:
        gc = gc + jnp.where(rows >= sh, pltpu.roll(gc, sh, 0), 0.0)
        sh *= 2
    r = heads * t
    stack = lambda f: jnp.concatenate([f(h) for h in range(heads)], axis=0)
    l2n = lambda a: a * lax.rsqrt(jnp.sum(a * a, axis=-1, keepdims=True) + RMS_EPS)
    q = stack(lambda h: l2n(qkv[:, h * dk:(h + 1) * dk]) * (dk ** -0.5))
    k = stack(lambda h: l2n(qkv[:, nqk + h * dk:nqk + (h + 1) * dk]))
    v = stack(lambda h: qkv[:, 2 * nqk + h * dk:2 * nqk + (h + 1) * dk])
    beta_c = stack(lambda h: beta[:, h:h + 1])
    gc_c = stack(lambda h: gc[:, h:h + 1])
    row = _iota2((r, r), 0)
    col = _iota2((r, r), 1)
    gc_r = jnp.sum(jnp.where(row == col, gc_c, 0.0), axis=0, keepdims=True)
    same = (row // t) == (col // t)
    decay = jnp.where(same & (row >= col), jnp.exp(gc_c - gc_r), 0.0)
    eg = jnp.exp(gc_c)
    kb = k * beta_c
    kc = k.astype(cdt)
    a_mat = jnp.where(same & (row > col), _dot_nt(kb.astype(cdt), kc) * decay, 0.0)
    y = jnp.concatenate([v * beta_c, kb * eg], axis=-1)
    p = -a_mat
    nsteps = max(1, (t - 1).bit_length())
    for i in range(nsteps):
        pc = p.astype(cdt)
        y = y + _dot(pc, y.astype(cdt))
        if i + 1 < nsteps:
            p = _dot(pc, pc)
    u = y[:, :dk]
    w = y[:, dk:]
    qe = q * eg
    ws_qs = [_dot(jnp.concatenate([w[h * t:(h + 1) * t], qe[h * t:(h + 1) * t]], axis=0).astype(cdt),
                  s_scr[h].astype(cdt)) for h in range(heads)]
    v_new = u - jnp.concatenate([a[:t] for a in ws_qs], axis=0)
    attn = _dot_nt(q.astype(cdt), kc) * decay
    o = jnp.concatenate([a[t:] for a in ws_qs], axis=0) + _dot(attn.astype(cdt), v_new.astype(cdt))
    o = o * lax.rsqrt(jnp.mean(o * o, axis=-1, keepdims=True) + RMS_EPS) * norm_g
    z = z_ref[...]
    for h in range(heads):
        g_last = gc[t - 1:t, h:h + 1]
        k_dec = (k[h * t:(h + 1) * t] * jnp.exp(g_last - gc[:, h:h + 1])).astype(cdt)
        s_scr[h] = s_scr[h] * jnp.exp(g_last) + _dot_tn(k_dec, v_new[h * t:(h + 1) * t].astype(cdt))
        mixed_ref[:, h * dk:(h + 1) * dk] = jnp.where(rowmask, o[h * t:(h + 1) * t] * _silu(z[:, h * dk:(h + 1) * dk]), 0.0)

    xb = jnp.where(rowmask, xb_ref[...], 0.0)
    xc = _causal_conv(xb_ext, xb, lw_ref[...], t) + lv_ref[0:1, :]

    @pl.when(c == c_last)
    def _():
        _put_state(lconv_out, ow, xb_ext[pl.ds(EXT0 + v_last - HIST, HIST), :])

    xb_ext[pl.ds(EXT0 - HIST, HIST), :] = xb_ext[pl.ds(EXT0 + t - HIST, HIST), :]
    xcb = xc.astype(BF16)
    r = jax.nn.sigmoid(_dot(xcb, wa_ref[...]) + lv_ref[1:2, :])
    gi = jax.nn.sigmoid(_dot(xcb, wx_ref[...]) + lv_ref[2:3, :])
    log_a = jnp.where(rowmask, -LRU_C * r * jax.nn.softplus(-lv_ref[3:4, :]), 0.0)
    a = jnp.exp(log_a)
    uu = jnp.where(rowmask, jnp.sqrt(1.0 - a * a) * (gi * xc), 0.0)
    rows = _iota2((t, 1), 0)
    sh = 1
    while sh < t:
        keep = rows >= sh
        uu = jnp.where(keep, a * pltpu.roll(uu, sh, 0) + uu, uu)
        a = jnp.where(keep, a * pltpu.roll(a, sh, 0), a)
        sh *= 2
    hh = uu + a * h_scr[...]
    h_scr[...] = hh[t - 1:t, :]
    yb = yb_ref[...]
    gelu = 0.5 * yb * (1.0 + jnp.tanh(math.sqrt(2.0 / math.pi) * (yb + 0.044715 * (yb * yb * yb))))
    mixed_ref[:, nqk:] = jnp.where(rowmask, gelu * hh, 0.0)

    @pl.when(c == c_last)
    def _():
        _put_state(s_out, ow, s_scr[...])
        _put_state(h_out, ow, h_scr[...])


def _state_out_specs(shapes, prev, lo, n_l, batch):
    lead = n_l if prev is None else 1
    first = 0 if prev is None else lo
    specs = tuple(pl.BlockSpec((lead, 1) + sh, functools.partial(lambda b, c, nz: (first, b) + (0,) * nz, nz=len(sh)))
                  for sh in shapes)
    out_shape = tuple(jax.ShapeDtypeStruct((n_l, batch) + sh, F32) for sh in shapes)
    return specs, out_shape, (lo if prev is None else 0)


def _ab_mixer(proj, dconv0, s0, lconv0, h0, wts, prev, *, t, length, batch, cpb, rb0, li, lo, n_l):
    dnw, dnv, lw, lv, wa, wx = wts
    heads = s0.shape[2]
    nqk = heads * LANE
    nv = heads * s0.shape[4]
    lw_ch = lw.shape[1]
    dmix = nv + lw_ch
    rmap = lambda j: (lambda b, c: (rb0 + b * cpb + c, j))
    full2 = lambda b, c: (0, 0)
    in_specs = [
        pl.BlockSpec((t, 3 * nqk), rmap(0)),
        pl.BlockSpec((t, nv), rmap(3 * nqk // nv)),
        pl.BlockSpec((t, lw_ch), rmap((3 * nqk + nv) // lw_ch)),
        pl.BlockSpec((t, lw_ch), rmap((3 * nqk + nv) // lw_ch + 1)),
        pl.BlockSpec((t, 2 * LANE), rmap((3 * nqk + nv + 2 * lw_ch) // (2 * LANE))),
        pl.BlockSpec((1, 1, HIST, 3 * nqk), lambda b, c: (li, b, 0, 0)),
        pl.BlockSpec((1, 1) + s0.shape[2:], lambda b, c: (li, b, 0, 0, 0)),
        pl.BlockSpec((1, 1, HIST, lw_ch), lambda b, c: (li, b, 0, 0)),
        pl.BlockSpec((1, 1, lw_ch), lambda b, c: (b, 0, 0)),
        pl.BlockSpec(dnw.shape, full2), pl.BlockSpec(dnv.shape, full2),
        pl.BlockSpec(lw.shape, full2), pl.BlockSpec(lv.shape, full2),
        pl.BlockSpec(wa.shape, full2), pl.BlockSpec(wx.shape, full2),
    ]
    args = [proj, proj, proj, proj, proj, dconv0, s0, lconv0, h0, dnw, dnv, lw, lv, wa, wx]
    st_specs, st_shapes, ow = _state_out_specs(
        (s0.shape[2:], (HIST, 3 * nqk), (1, lw_ch), (HIST, lw_ch)), prev, lo, n_l, batch)
    aliases = {}
    if prev is not None:
        in_specs += [pl.BlockSpec(memory_space=pl.ANY)] * len(prev)
        aliases = {len(args) + j: 1 + j for j in range(len(prev))}
        args += list(prev)
    return pl.pallas_call(
        functools.partial(_ab_kernel, t=t, length=length, heads=heads, ow=ow),
        grid=(batch, cpb),
        in_specs=in_specs,
        out_specs=(pl.BlockSpec((t, dmix), lambda b, c: (b * cpb + c, 0)),) + st_specs,
        out_shape=(jax.ShapeDtypeStruct((batch * cpb * t, dmix), F32),) + st_shapes,
        scratch_shapes=[pltpu.VMEM((t + EXT0, 3 * nqk), F32), pltpu.VMEM((t + EXT0, lw_ch), F32),
                        pltpu.VMEM(s0.shape[2:], F32), pltpu.VMEM((1, lw_ch), F32)],
        input_output_aliases=aliases,
        compiler_params=pltpu.CompilerParams(dimension_semantics=("parallel", "arbitrary"),
                                             vmem_limit_bytes=VMEM_LIMIT),
        name="ab_mixer")(*args)


def _ssd_kernel(z_ref, xs_ref, bc_ref, dt_ref, xconv0_ref, bconv0_ref, s0_ref,
                wxs_ref, wbc_ref, bxs_ref, bbc_ref, vec_ref, ng_ref, *rest,
                t, length, heads, groups, ow):
    mixed_ref, s_out, conv_out, xs_ext, bc_ext, s_scr, ybuf = rest[-7:]
    c = pl.program_id(1)
    c_last = (length - 1) // t
    v_last = length - c_last * t
    inner = xs_ref.shape[1]
    p = inner // heads
    n = s_scr.shape[2]
    hpg = heads // groups
    gw = inner // groups
    cdt = BF16 if t % 16 == 0 else F32

    @pl.when(c == 0)
    def _():
        xs_ext[pl.ds(0, EXT0), :] = jnp.zeros((EXT0, inner), F32)
        bc_ext[pl.ds(0, EXT0), :] = jnp.zeros((EXT0, inner), F32)
        xs_ext[pl.ds(EXT0 - HIST, HIST), :] = xconv0_ref[0, 0]
        bc_ext[pl.ds(EXT0 - HIST, HIST), :] = bconv0_ref[0, 0]
        s_scr[...] = s0_ref[0, 0]

    rowmask = _iota2((t, 1), 0) < (length - c * t)
    xs = _silu(_causal_conv(xs_ext, jnp.where(rowmask, xs_ref[...], 0.0), wxs_ref[...], t) + bxs_ref[...])
    bc = _silu(_causal_conv(bc_ext, jnp.where(rowmask, bc_ref[...], 0.0), wbc_ref[...], t) + bbc_ref[...])

    @pl.when(c == c_last)
    def _():
        _put_state(conv_out, ow, xs_ext[pl.ds(EXT0 + v_last - HIST, HIST), :], (slice(None), slice(0, inner)))
        _put_state(conv_out, ow, bc_ext[pl.ds(EXT0 + v_last - HIST, HIST), :], (slice(None), slice(inner, 2 * inner)))

    xs_ext[pl.ds(EXT0 - HIST, HIST), :] = xs_ext[pl.ds(EXT0 + t - HIST, HIST), :]
    bc_ext[pl.ds(EXT0 - HIST, HIST), :] = bc_ext[pl.ds(EXT0 + t - HIST, HIST), :]

    dt = jnp.where(rowmask, jax.nn.softplus(jnp.where(rowmask, dt_ref[...], 0.0) + vec_ref[0:1, :]), 0.0)
    da = dt * (-jnp.exp(vec_ref[1:2, :]))
    d_skip = vec_ref[2:3, :]
    cs, cst = _cumsum_rows(da)
    ecs = jnp.exp(cs)
    e_last = jnp.exp(cs[t - 1:t, :])
    w_dec = jnp.exp(cs[t - 1:t, :] - cs)
    causal = _iota2((t, t), 0) >= _iota2((t, t), 1)
    gn = groups * n
    lane_g = _iota2((t, gw), 1)
    row_g = _iota2((gw, 1), 0)
    for g in range(groups):
        e0 = g * hpg
        bm = bc[:, g * n:(g + 1) * n].astype(cdt)
        cm = bc[:, gn + g * n:gn + (g + 1) * n].astype(cdt)
        cb = _dot_nt(cm, bm)
        xg = xs[:, g * gw:(g + 1) * gw]
        xdt = xg * _expand_heads(dt, e0, hpg, p)
        m_cat = jnp.concatenate(
            [(cb * jnp.where(causal, jnp.exp(cs[:, e0 + j:e0 + j + 1] - cst[e0 + j:e0 + j + 1, :]), 0.0)).astype(cdt)
             for j in range(hpg)], axis=1)
        x_bd = jnp.concatenate(
            [jnp.where((lane_g >= j * p) & (lane_g < (j + 1) * p), xdt, 0.0) for j in range(hpg)], axis=0).astype(cdt)
        sg = s_scr[pl.ds(e0, hpg)].reshape(gw, n)
        y = _dot(m_cat, x_bd) + _dot_nt(cm, sg.astype(cdt)) * _expand_heads(ecs, e0, hpg, p)
        el = e_last[:, e0:e0 + 1]
        for j in range(1, hpg):
            el = jnp.where(row_g >= j * p, e_last[:, e0 + j:e0 + j + 1], el)
        s_new = sg * el + _dot_tn((xdt * _expand_heads(w_dec, e0, hpg, p)).astype(cdt), bm)
        s_scr[pl.ds(e0, hpg)] = s_new.reshape(hpg, p, n)
        ybuf[:, g * gw:(g + 1) * gw] = y + xg * _expand_heads(d_skip, e0, hpg, p)
    yz = ybuf[...] * _silu(z_ref[...])
    ng = ng_ref[...]
    for g in range(groups):
        seg = yz[:, g * gw:(g + 1) * gw]
        seg = seg * lax.rsqrt(jnp.mean(seg * seg, axis=-1, keepdims=True) + RMS_EPS) * ng[:, g * gw:(g + 1) * gw]
        mixed_ref[:, g * gw:(g + 1) * gw] = jnp.where(rowmask, seg, 0.0)

    @pl.when(c == c_last)
    def _():
        _put_state(s_out, ow, s_scr[...])


def _ssd_mixer(proj, conv0, s0, wts, prev, *, t, length, batch, cpb, rb0, groups, li, lo, n_l):
    cw, cb, vec, ng = wts
    heads = s0.shape[2]
    inner = heads * s0.shape[3]
    rmap = lambda j: (lambda b, c: (rb0 + b * cpb + c, j))
    in_specs = [
        pl.BlockSpec((t, inner), rmap(0)), pl.BlockSpec((t, inner), rmap(1)), pl.BlockSpec((t, inner), rmap(2)),
        pl.BlockSpec((t, LANE), rmap(3 * inner // LANE)),
        pl.BlockSpec((1, 1, HIST, inner), lambda b, c: (li, b, 0, 0)),
        pl.BlockSpec((1, 1, HIST, inner), lambda b, c: (li, b, 0, 1)),
        pl.BlockSpec((1, 1) + s0.shape[2:], lambda b, c: (li, b, 0, 0, 0)),
        pl.BlockSpec((CONV_W, inner), lambda b, c: (0, 0)), pl.BlockSpec((CONV_W, inner), lambda b, c: (0, 1)),
        pl.BlockSpec((1, inner), lambda b, c: (0, 0)), pl.BlockSpec((1, inner), lambda b, c: (0, 1)),
        pl.BlockSpec(vec.shape, lambda b, c: (0, 0)), pl.BlockSpec(ng.shape, lambda b, c: (0, 0)),
    ]
    args = [proj, proj, proj, proj, conv0, conv0, s0, cw, cw, cb, cb, vec, ng]
    st_specs, st_shapes, ow = _state_out_specs((s0.shape[2:], (HIST, conv0.shape[3])), prev, lo, n_l, batch)
    aliases = {}
    if prev is not None:
        in_specs += [pl.BlockSpec(memory_space=pl.ANY)] * len(prev)
        aliases = {len(args) + j: 1 + j for j in range(len(prev))}
        args += list(prev)
    return pl.pallas_call(
        functools.partial(_ssd_kernel, t=t, length=length, heads=heads, groups=groups, ow=ow),
        grid=(batch, cpb),
        in_specs=in_specs,
        out_specs=(pl.BlockSpec((t, inner), lambda b, c: (b * cpb + c, 0)),) + st_specs,
        out_shape=(jax.ShapeDtypeStruct((batch * cpb * t, inner), F32),) + st_shapes,
        scratch_shapes=[pltpu.VMEM((t + EXT0, inner), F32), pltpu.VMEM((t + EXT0, inner), F32),
                        pltpu.VMEM(s0.shape[2:], F32), pltpu.VMEM((t, inner), F32)],
        input_output_aliases=aliases,
        compiler_params=pltpu.CompilerParams(dimension_semantics=("parallel", "arbitrary"),
                                             vmem_limit_bytes=VMEM_LIMIT),
        name="ssd_mixer")(*args)


def _layer_norm(h, g, b):
    mu = jnp.mean(h, axis=-1, keepdims=True)
    d = h - mu
    var = jnp.mean(d * d, axis=-1, keepdims=True)
    return d * lax.rsqrt(var + LN_EPS) * g + b


def _col_to_rows(v):
    eye = _iota2((LANE, LANE), 0) == _iota2((LANE, LANE), 1)
    rows = [jnp.sum(jnp.where(eye, v[i * LANE:(i + 1) * LANE, :], 0.0), axis=0, keepdims=True)
            for i in range(v.shape[0] // LANE)]
    return jnp.concatenate(rows, axis=0)


def _outln_kernel(yp_ref, ys_ref, x_ref, w_ref, ln_ref, wr_ref, br_ref, o_ref, bkt_ref, rank_ref, cnt_ref,
                  ybuf, cnt_scr, *, alpha, n_groups, epg, nbp, n_buckets):
    d = x_ref.shape[1]
    i = pl.program_id(0)

    @pl.when(i == 0)
    def _():
        cnt_scr[...] = jnp.zeros_like(cnt_scr)

    @pl.when(i < nbp)
    def _():
        ybuf[...] = yp_ref[...].astype(BF16)

    @pl.when(i >= nbp)
    def _():
        ybuf[...] = ys_ref[...].astype(BF16)

    acc = _dot(ybuf[...], w_ref[...])
    xn = _layer_norm(alpha * x_ref[...] + acc, ln_ref[0:1, :], ln_ref[1:2, :])
    o_ref[:, :d] = xn
    xh = xn.astype(BF16)
    xl = (xn - xh.astype(F32)).astype(BF16)
    hw = _dot(xh, wr_ref[...])
    logits = hw[:, :ROUTE_LANES] + hw[:, ROUTE_LANES:] + _dot(xl, wr_ref[:, :ROUTE_LANES]) + br_ref[...]
    tm = logits.shape[0]
    lane = _iota2((tm, ROUTE_LANES), 1)
    ninf = -jnp.inf
    gl = jnp.where(lane < n_groups, logits, ninf)
    gmax = jnp.max(gl, axis=-1, keepdims=True)
    gidx = jnp.min(jnp.where(gl == gmax, lane, ROUTE_LANES), axis=-1, keepdims=True)
    g_w = 1.0 / jnp.sum(jnp.where(lane < n_groups, jnp.exp(logits - gmax), 0.0), axis=-1, keepdims=True)
    lo = n_groups + epg * gidx
    el = jnp.where((lane >= lo) & (lane < lo + epg), logits, ninf)
    m1 = jnp.max(el, axis=-1, keepdims=True)
    i1 = jnp.min(jnp.where(el == m1, lane, ROUTE_LANES), axis=-1, keepdims=True)
    el2 = jnp.where(lane == i1, ninf, el)
    m2 = jnp.max(el2, axis=-1, keepdims=True)
    i2 = jnp.min(jnp.where(el2 == m2, lane, ROUTE_LANES), axis=-1, keepdims=True)
    r = jnp.exp(m2 - m1)
    w1 = g_w / (1.0 + r)
    w2 = g_w * r / (1.0 + r)
    e1 = i1 - lo
    e2 = i2 - lo
    first = e1 < e2
    ea = jnp.where(first, e1, e2)
    eb = jnp.where(first, e2, e1)
    bucket = gidx * PAIR_SLOTS + ea * epg + eb
    ga = jnp.where(first, w1, w2)
    gb = jnp.where(first, w2, w1)
    o_ref[:, d:] = jnp.where(lane == 1, ga, jnp.where(lane == 2, gb, 0.0))
    onehot = _iota2((tm, n_buckets), 1) == bucket
    oh = jnp.where(onehot, 1.0, 0.0)
    earlier = jnp.where(_iota2((tm, tm), 0) > _iota2((tm, tm), 1), 1.0, 0.0).astype(BF16)
    before = _dot(earlier, oh.astype(BF16)) + cnt_scr[...]
    rank = jnp.sum(jnp.where(onehot, before, 0.0), axis=-1, keepdims=True)
    cnt_scr[...] = cnt_scr[...] + jnp.sum(oh, axis=0, keepdims=True)
    bkt_ref[0] = _col_to_rows(bucket.astype(F32)).astype(jnp.int32)
    rank_ref[0] = _col_to_rows(rank).astype(jnp.int32)
    cnt_ref[...] = jnp.broadcast_to(cnt_scr[...], cnt_ref.shape).astype(jnp.int32)


def _outln(mixed_p, mixed_s, x, w, ln, wr, br, *, n, tm, alpha, n_groups, epg, n_buckets):
    d = x.shape[1]
    dm = mixed_p.shape[1]
    nbp = mixed_p.shape[0] // tm
    nb = n // tm
    return pl.pallas_call(
        functools.partial(_outln_kernel, alpha=alpha, n_groups=n_groups, epg=epg, nbp=nbp, n_buckets=n_buckets),
        grid=(nb,),
        in_specs=[pl.BlockSpec((tm, dm), lambda i: (jnp.minimum(i, nbp - 1), 0)),
                  pl.BlockSpec((tm, dm), lambda i: (jnp.maximum(i - nbp, 0), 0)),
                  pl.BlockSpec((tm, d), lambda i: (i, 0)),
                  pl.BlockSpec((dm, d), lambda i: (0, 0), pipeline_mode=pl.Buffered(1)),
                  pl.BlockSpec(ln.shape, lambda i: (0, 0)),
                  pl.BlockSpec(wr.shape, lambda i: (0, 0)),
                  pl.BlockSpec(br.shape, lambda i: (0, 0))],
        out_specs=(pl.BlockSpec((tm, d + ROUTE_LANES), lambda i: (i, 0)),
                   pl.BlockSpec((1, tm // LANE, LANE), lambda i: (i, 0, 0)),
                   pl.BlockSpec((1, tm // LANE, LANE), lambda i: (i, 0, 0)),
                   pl.BlockSpec((8, n_buckets), lambda i: (0, 0))),
        out_shape=(jax.ShapeDtypeStruct((n, d + ROUTE_LANES), F32),
                   jax.ShapeDtypeStruct((nb, tm // LANE, LANE), jnp.int32),
                   jax.ShapeDtypeStruct((nb, tm // LANE, LANE), jnp.int32),
                   jax.ShapeDtypeStruct((8, n_buckets), jnp.int32)),
        scratch_shapes=[pltpu.VMEM((tm, dm), BF16), pltpu.VMEM((1, n_buckets), F32)],
        compiler_params=pltpu.CompilerParams(dimension_semantics=("arbitrary",), vmem_limit_bytes=VMEM_LIMIT),
        name="outproj_ln_route")(mixed_p, mixed_s, x, w, ln, wr, br)


def _moe_kernel(c8_ref, ea_ref, eb_ref, src_ref, nxt_ref, x_hbm, w1a_ref, w3a_ref, w2a_ref, w1b_ref, w3b_ref,
                w2b_ref, ln_ref, o_hbm, xbuf, obuf, gsem, ssem, *, alpha, tr, n_rows):
    t = pl.program_id(0)
    nt = pl.num_programs(0)
    d = obuf.shape[2]
    slot = lax.rem(t, 2)
    other = 1 - slot
    c8_t = c8_ref[t]
    c8_next = jnp.where(t + 1 < nt, c8_ref[jnp.minimum(t + 1, nt - 1)], 0)

    def gather(idx_ref, s, c8):
        for g in range(tr // DMA_ROWS):
            @pl.when(g < c8)
            def _():
                for u in range(DMA_ROWS):
                    r = g * DMA_ROWS + u
                    pltpu.make_async_copy(x_hbm.at[pl.ds(jnp.minimum(idx_ref[0, 0, r], n_rows - 1), 1), :],
                                          xbuf.at[s, pl.ds(r, 1), :], gsem.at[s]).start(priority=u % 2)

    def scatter(s, c8):
        for g in range(tr // DMA_ROWS):
            @pl.when(g < c8)
            def _():
                for u in range(DMA_ROWS):
                    r = g * DMA_ROWS + u
                    pltpu.make_async_copy(obuf.at[s, pl.ds(r, 1), :],
                                          o_hbm.at[pl.ds(src_ref[0, 0, r], 1), :], ssem.at[s]).start(priority=u % 2)

    def wait_groups(desc, c8):
        def body(i, carry):
            desc.wait()
            return carry
        lax.fori_loop(0, c8, body, 0)

    def wait_gather(s, c8):
        wait_groups(pltpu.make_async_copy(x_hbm.at[pl.ds(0, DMA_ROWS), :], xbuf.at[s, pl.ds(0, DMA_ROWS), :],
                                          gsem.at[s]), c8)

    def wait_scatter(s, c8):
        wait_groups(pltpu.make_async_copy(obuf.at[s, pl.ds(0, DMA_ROWS), :], o_hbm.at[pl.ds(0, DMA_ROWS), :],
                                          ssem.at[s]), c8)

    @pl.when(t == 0)
    def _():
        xbuf[...] = jnp.zeros_like(xbuf)
        obuf[...] = jnp.zeros_like(obuf)
        for s in range(2):
            fill = pltpu.make_async_copy(obuf.at[s], o_hbm.at[pl.ds(n_rows + s * tr, tr), :], ssem.at[s])
            fill.start()
            fill.wait()
        gather(src_ref, 0, c8_t)

    gather(nxt_ref, other, c8_next)

    @pl.when(c8_t > 0)
    def _():
        wait_gather(slot, c8_t)
        x = xbuf[slot, :, :d]
        ga = xbuf[slot, :, d + 1:d + 2]
        gb = xbuf[slot, :, d + 2:d + 3]
        xb = x.astype(BF16)
        ha = _silu(_dot(xb, w1a_ref[0])) * _dot(xb, w3a_ref[0])
        hb = _silu(_dot(xb, w1b_ref[0])) * _dot(xb, w3b_ref[0])
        ffn = _dot((ha * ga).astype(BF16), w2a_ref[0]) + _dot((hb * gb).astype(BF16), w2b_ref[0])
        y = _layer_norm(alpha * x + ffn, ln_ref[0:1, :], ln_ref[1:2, :])

        @pl.when(t >= 2)
        def _():
            wait_scatter(slot, c8_ref[jnp.maximum(t - 2, 0)])

        obuf[slot] = y
        scatter(slot, c8_t)

        @pl.when(c8_next == 0)
        def _():
            wait_scatter(slot, c8_t)

            @pl.when(t >= 1)
            def _():
                wait_scatter(other, c8_ref[jnp.maximum(t - 1, 0)])


def _moe(x1r, src, c8, ea, eb, w1, w3, w2, ln, *, alpha, tr):
    n, dr = x1r.shape
    d = dr - ROUTE_LANES
    nt = c8.shape[0]
    f = w1.shape[2]
    wa = lambda t, v, a, b: (a[t], 0, 0)
    wb = lambda t, v, a, b: (b[t], 0, 0)
    grid_spec = pltpu.PrefetchScalarGridSpec(
        num_scalar_prefetch=3,
        grid=(nt,),
        in_specs=[pl.BlockSpec((1, 1, tr), lambda t, v, a, b: (t, 0, 0), memory_space=pltpu.SMEM),
                  pl.BlockSpec((1, 1, tr), lambda t, v, a, b: (jnp.minimum(t + 1, nt - 1), 0, 0),
                               memory_space=pltpu.SMEM),
                  pl.BlockSpec(memory_space=pl.ANY),
                  pl.BlockSpec((1, d, f), wa), pl.BlockSpec((1, d, f), wa), pl.BlockSpec((1, f, d), wa),
                  pl.BlockSpec((1, d, f), wb), pl.BlockSpec((1, d, f), wb), pl.BlockSpec((1, f, d), wb),
                  pl.BlockSpec(ln.shape, lambda t, v, a, b: (0, 0))],
        out_specs=pl.BlockSpec(memory_space=pl.ANY),
        scratch_shapes=[pltpu.VMEM((2, tr, dr), F32), pltpu.VMEM((2, tr, d), F32),
                        pltpu.SemaphoreType.DMA((2,)), pltpu.SemaphoreType.DMA((2,))])
    return pl.pallas_call(
        functools.partial(_moe_kernel, alpha=alpha, tr=tr, n_rows=n),
        grid_spec=grid_spec,
        out_shape=jax.ShapeDtypeStruct((n + 2 * tr, d), F32),
        compiler_params=pltpu.CompilerParams(dimension_semantics=("arbitrary",), vmem_limit_bytes=VMEM_LIMIT),
        name="moe_routed")(c8, ea, eb, src, src, x1r, w1, w3, w2, w1, w3, w2, ln)


def _route_schedule(bucket, rank, counts, *, n, tr, nt, epg, e_off):
    nbk = counts.shape[0]
    tiles_b = (counts + tr - 1) // tr
    tile_end = jnp.cumsum(tiles_b)
    pstart = (tile_end - tiles_b) * tr
    hit = bucket[:, None] == jnp.arange(nbk, dtype=jnp.int32)[None, :]
    dest = jnp.sum(jnp.where(hit, pstart[None, :], 0), axis=1) + rank
    slots = jnp.arange(nt * tr, dtype=jnp.int32)
    trash = n + ((slots // tr) % 2) * tr + slots % tr
    src = trash.at[dest].set(jnp.arange(n, dtype=jnp.int32))
    tiles = jnp.arange(nt, dtype=jnp.int32)
    tb = jnp.sum(tile_end[None, :] <= jnp.minimum(tiles, tile_end[-1] - 1)[:, None], axis=1).astype(jnp.int32)
    own = tb[:, None] == jnp.arange(nbk, dtype=jnp.int32)[None, :]
    pick = lambda tab: jnp.sum(jnp.where(own, tab[None, :], 0), axis=1)
    rows_t = jnp.clip(pick(counts) - (tiles - pick(tile_end - tiles_b)) * tr, 0, tr)
    c8 = jnp.where(tiles < tile_end[-1], (rows_t + DMA_ROWS - 1) // DMA_ROWS, 0)
    grp = tb // PAIR_SLOTS
    pair = tb % PAIR_SLOTS
    ea = e_off + grp * epg + pair // epg
    eb = e_off + grp * epg + pair % epg
    return src.reshape(nt, 1, tr), c8.astype(jnp.int32), ea.astype(jnp.int32), eb.astype(jnp.int32)


def _pad_cols(w, width):
    return jnp.pad(w, ((0, 0), (0, width - w.shape[1])))


def _pad_vec(v, width=LANE):
    return jnp.pad(v.astype(F32), (0, width - v.shape[0]))[None, :]


def _rows8(*vecs):
    m = jnp.stack([v.astype(F32) for v in vecs], axis=0)
    return jnp.pad(m, ((0, 8 - m.shape[0]), (0, 0)))


def _block_diag(w):
    nb, bd, _ = w.shape
    eye = jnp.eye(nb, dtype=w.dtype)
    return jnp.einsum("nde,nm->ndme", w, eye).reshape(nb * bd, nb * bd)


def kernel(x_prompt, x_sample, state_delta, state_delta_conv, state_lru, state_lru_conv, state_ssm, state_ssm_conv,
           meta, w_in_ab, dn_conv_w, dn_a_log, dn_dt_bias, dn_norm_g, lru_conv_w, lru_conv_b, lru_w_a, lru_b_a,
           lru_w_x, lru_b_x, lru_lam, w_out_ab, w_in_ssd, ssd_conv_w, ssd_conv_b, ssd_dt_bias, ssd_a_log, ssd_d,
           ssd_norm_g, w_out_ssd, ln_g, ln_b, moe_w_group, moe_b_group, moe_w_expert, moe_b_expert, moe_w1, moe_w3,
           moe_w2):
    bp, seq, d = x_prompt.shape
    bs, ls, _ = x_sample.shape
    depth = ln_g.shape[0]
    alpha = (2.0 * depth) ** 0.25
    n_groups, epg = moe_w1.shape[1], moe_w1.shape[2]
    heads_dn = dn_a_log.shape[1]
    nqk = heads_dn * state_delta.shape[3]
    nv = heads_dn * state_delta.shape[4]
    lru_w = lru_lam.shape[1]
    heads_ssd = ssd_a_log.shape[1]
    inner = heads_ssd * state_ssm.shape[3]
    n_state = state_ssm.shape[4]
    ssd_groups = (ssd_conv_w.shape[2] - inner) // (2 * n_state)
    assert nqk == nv and state_delta.shape[3] == LANE and state_delta.shape[4] == LANE and n_state == LANE

    lp_len = N_META + seq
    chunk_lcm = math.lcm(DN_CHUNK, SSD_CHUNK)
    lpad = -(-lp_len // chunk_lcm) * chunk_lcm
    np_rows = bp * lpad
    ns_rows = bs * ls
    ntot = np_rows + ns_rows
    tm = math.gcd(math.gcd(np_rows, ns_rows), ROW_TILE)
    assert tm % LANE == 0 and np_rows % ls == 0 and ls % 8 == 0 and ls >= HIST
    assert (lp_len - 1) % DN_CHUNK + 1 >= HIST and (lp_len - 1) % SSD_CHUNK + 1 >= HIST

    xp = jnp.concatenate([jnp.broadcast_to(meta.astype(F32), (bp, N_META, d)), x_prompt,
                          jnp.zeros((bp, lpad - lp_len, d), F32)], axis=1)
    x = jnp.concatenate([xp.reshape(np_rows, d), x_sample.reshape(ns_rows, d)], axis=0)

    tr = MOE_TILE
    n_buckets = n_groups * PAIR_SLOTS
    n_pairs = n_groups * (epg * (epg - 1) // 2)
    nt = -(-(ntot + n_pairs * (tr - 1)) // tr)
    n_exp = n_groups * epg
    w1 = moe_w1.reshape(depth * n_exp, d, -1).astype(BF16)
    w3 = moe_w3.reshape(depth * n_exp, d, -1).astype(BF16)
    w2 = moe_w2.reshape(depth * n_exp, -1, d).astype(BF16)

    zeros = lambda *s: jnp.zeros(s, F32)
    n_ab, n_ssd = (depth + 1) // 2, depth // 2
    st_ab_p = st_ab_s = st_ssd_p = st_ssd_s = None
    for layer in range(depth):
        i = layer // 2
        if layer % 2 == 0:
            wi = w_in_ab[i]
            o_b = 3 * nqk + nv
            w_in = jnp.concatenate([wi[:, :o_b], wi[:, o_b + 2 * heads_dn:],
                                    _pad_cols(wi[:, o_b:o_b + heads_dn], LANE),
                                    _pad_cols(wi[:, o_b + heads_dn:o_b + 2 * heads_dn], LANE)], axis=1).astype(BF16)
            proj = _inproj(x, w_in, tm, ntot)
            wts = (dn_conv_w[i], _rows8(_pad_vec(dn_a_log[i])[0], _pad_vec(dn_dt_bias[i])[0], dn_norm_g[i]),
                   lru_conv_w[i], _rows8(lru_conv_b[i], lru_b_a[i], lru_b_x[i], lru_lam[i]),
                   _block_diag(lru_w_a[i]).astype(BF16), _block_diag(lru_w_x[i]).astype(BF16))
            mixed_p, *st_ab_p = _ab_mixer(
                proj, zeros(1, bp, HIST, 3 * nqk), zeros(1, bp, *state_delta.shape[2:]), zeros(1, bp, HIST, lru_w),
                zeros(bp, 1, lru_w), wts, st_ab_p, t=DN_CHUNK, length=lp_len, batch=bp, cpb=lpad // DN_CHUNK, rb0=0,
                li=0, lo=i, n_l=n_ab)
            mixed_s, *st_ab_s = _ab_mixer(
                proj, state_delta_conv, state_delta, state_lru_conv, state_lru[i][:, None, :], wts, st_ab_s,
                t=ls, length=ls, batch=bs, cpb=1, rb0=np_rows // ls, li=i, lo=i, n_l=n_ab)
            w_out = w_out_ab[i].astype(BF16)
        else:
            wi = w_in_ssd[i]
            e_raw = wi.shape[1]
            w_in = _pad_cols(wi, -(-e_raw // LANE) * LANE).astype(BF16)
            proj = _inproj(x, w_in, tm, ntot)
            wts = (ssd_conv_w[i], ssd_conv_b[i][None, :],
                   _rows8(_pad_vec(ssd_dt_bias[i])[0], _pad_vec(ssd_a_log[i])[0], _pad_vec(ssd_d[i])[0]),
                   ssd_norm_g[i][None, :])
            mixed_p, *st_ssd_p = _ssd_mixer(
                proj, zeros(1, bp, HIST, ssd_conv_w.shape[2]), zeros(1, bp, *state_ssm.shape[2:]), wts, st_ssd_p,
                t=SSD_CHUNK, length=lp_len, batch=bp, cpb=lpad // SSD_CHUNK, rb0=0, groups=ssd_groups,
                li=0, lo=i, n_l=n_ssd)
            mixed_s, *st_ssd_s = _ssd_mixer(
                proj, state_ssm_conv, state_ssm, wts, st_ssd_s,
                t=ls, length=ls, batch=bs, cpb=1, rb0=np_rows // ls, groups=ssd_groups, li=i, lo=i, n_l=n_ssd)
            w_out = w_out_ssd[i].astype(BF16)
        wr = _pad_cols(jnp.concatenate([moe_w_group[layer], moe_w_expert[layer]], axis=1), ROUTE_LANES)
        wr_hi = wr.astype(BF16)
        wr = jnp.concatenate([wr_hi, (wr - wr_hi.astype(F32)).astype(BF16)], axis=1)
        br = _pad_vec(jnp.concatenate([moe_b_group[layer], moe_b_expert[layer]]), ROUTE_LANES)
        x1r, bucket, rank, counts = _outln(mixed_p, mixed_s, x, w_out, _rows8(ln_g[layer, 0], ln_b[layer, 0]), wr, br,
                                           n=ntot, tm=tm, alpha=alpha, n_groups=n_groups, epg=epg,
                                           n_buckets=n_buckets)
        src, c8, ea, eb = _route_schedule(bucket.reshape(ntot), rank.reshape(ntot), counts[0], n=ntot, tr=tr,
                                             nt=nt, epg=epg, e_off=layer * n_exp)
        x = _moe(x1r, src, c8, ea, eb, w1, w3, w2, _rows8(ln_g[layer, 1], ln_b[layer, 1]), alpha=alpha, tr=tr)

    y_prompt = x[:np_rows].reshape(bp, lpad, d)[:, N_META:lp_len]
    y_sample = x[np_rows:ntot].reshape(bs, ls, d)
    ab = lambda st: (st[0], st[1], st[2][:, :, 0], st[3])
    return (y_prompt, y_sample) + ab(st_ab_p) + tuple(st_ssd_p) + ab(st_ab_s) + tuple(st_ssd_s)
```

```python
import functools
import math

import jax
import jax.numpy as jnp
from jax import lax
from jax.experimental import pallas as pl
from jax.experimental.pallas import tpu as pltpu

F32 = jnp.float32
BF16 = jnp.bfloat16
HI = lax.Precision.HIGHEST

LN_EPS = 1e-5
RMS_EPS = 1e-6
LRU_C = 8.0
N_META = 16
CONV_W = 4
HIST = CONV_W - 1

LANE = 128
EXT0 = 8
DN_CHUNK = 64
DN_BLOCK = 128
SSD_CHUNK = 128
ROW_TILE = 512
MOE_TILE = 128
DMA_ROWS = 8
GATHER_AHEAD = 2
PAIR_SLOTS = 64
ROUTE_LANES = 128
VMEM_LIMIT = 56 * 1024 * 1024


def _silu(x):
    return x * jax.nn.sigmoid(x)


def _dot(a, b, **kw):
    return jnp.dot(a, b, preferred_element_type=F32, **kw)


def _dot_nt(a, b, **kw):
    return lax.dot_general(a, b, (((1,), (1,)), ((), ())), preferred_element_type=F32, **kw)


def _dot_tn(a, b, **kw):
    return lax.dot_general(a, b, (((0,), (0,)), ((), ())), preferred_element_type=F32, **kw)


def _iota2(shape, dim):
    return lax.broadcasted_iota(jnp.int32, shape, dim)


def _cumsum_rows(x):
    t = x.shape[0]
    rows = _iota2((t, 1), 0)
    cs = x
    sh = 1
    while sh < t:
        cs = cs + jnp.where(rows >= sh, pltpu.roll(cs, sh, 0), 0.0)
        sh *= 2
    if t == LANE:
        cst = cs.T
    else:
        eye = (_iota2((LANE, LANE), 0) == _iota2((LANE, LANE), 1)).astype(F32)
        cst = _dot_nt(eye, cs, precision=HI)
    return cs, cst


def _expand_heads(v, e0, nh, p):
    lane = _iota2((v.shape[0], nh * p), 1)
    out = jnp.broadcast_to(v[:, e0:e0 + 1], (v.shape[0], nh * p))
    for j in range(1, nh):
        out = jnp.where(lane >= j * p, v[:, e0 + j:e0 + j + 1], out)
    return out


def _put_state(ref, ow, val, sl=()):
    ref[(ow, 0) + sl] = val
    for j in range(ref.shape[0]):
        if j != ow:
            ref[(j, 0) + sl] = jnp.zeros_like(val)


def _causal_conv(ext_ref, x, w, t):
    ext_ref[pl.ds(EXT0, t), :] = x
    ext = ext_ref[...]
    y = x * w[HIST:CONV_W, :]
    for k in range(HIST):
        y = y + pltpu.roll(ext, HIST - k, 0)[EXT0:EXT0 + t, :] * w[k:k + 1, :]
    return y


def _inproj_kernel(x_ref, w_ref, o_ref, *, tn):
    xb = x_ref[...].astype(BF16)
    for j in range(o_ref.shape[1] // tn):
        o_ref[:, j * tn:(j + 1) * tn] = _dot(xb, w_ref[:, j * tn:(j + 1) * tn])


def _inproj(x, w, tm, n):
    k = x.shape[1]
    e = w.shape[1]
    tn = max(c for c in range(LANE, 1024 + LANE, LANE) if e % c == 0)
    return pl.pallas_call(
        functools.partial(_inproj_kernel, tn=tn),
        grid=(n // tm,),
        in_specs=[pl.BlockSpec((tm, k), lambda i: (i, 0)),
                  pl.BlockSpec((k, e), lambda i: (0, 0), pipeline_mode=pl.Buffered(1))],
        out_specs=pl.BlockSpec((tm, e), lambda i: (i, 0)),
        out_shape=jax.ShapeDtypeStruct((n, e), F32),
        compiler_params=pltpu.CompilerParams(dimension_semantics=("parallel",), vmem_limit_bytes=VMEM_LIMIT),
        name="inproj")(x, w)


def _ab_kernel(qkv_ref, z_ref, xb_ref, yb_ref, bg_ref, dconv0_ref, s0_ref, lconv0_ref, h0_ref,
               dnw_ref, dnv_ref, lw_ref, lv_ref, wa_ref, wx_ref, *rest,
               t, sub, length, heads, ow):
    mixed_ref, s_out, dconv_out, h_out, lconv_out, qkv_ext, xb_ext, s_scr, h_scr = rest[-9:]
    c = pl.program_id(1)
    c_last = (length - 1) // t
    v_last = length - c_last * t
    dk = LANE
    nqk = heads * dk
    cdt = BF16 if sub % 16 == 0 else F32

    @pl.when(c == 0)
    def _():
        qkv_ext[pl.ds(0, EXT0), :] = jnp.zeros((EXT0, qkv_ext.shape[1]), F32)
        xb_ext[pl.ds(0, EXT0), :] = jnp.zeros((EXT0, xb_ext.shape[1]), F32)
        qkv_ext[pl.ds(EXT0 - HIST, HIST), :] = dconv0_ref[0, 0]
        xb_ext[pl.ds(EXT0 - HIST, HIST), :] = lconv0_ref[0, 0]
        s_scr[...] = s0_ref[0, 0]
        h_scr[...] = h0_ref[0]

    rowmask = _iota2((t, 1), 0) < (length - c * t)

    x = jnp.where(rowmask, qkv_ref[...], 0.0)
    qkv = _silu(_causal_conv(qkv_ext, x, dnw_ref[...], t))

    @pl.when(c == c_last)
    def _():
        _put_state(dconv_out, ow, qkv_ext[pl.ds(EXT0 + v_last - HIST, HIST), :])

    qkv_ext[pl.ds(EXT0 - HIST, HIST), :] = qkv_ext[pl.ds(EXT0 + t - HIST, HIST), :]

    bg = jnp.where(rowmask, bg_ref[...], 0.0)
    a_log = dnv_ref[0:1, :]
    dt_bias = dnv_ref[1:2, :]
    norm_g = dnv_ref[2:3, :]
    beta = jnp.where(rowmask, jax.nn.sigmoid(bg[:, :LANE]), 0.0)
    g = jnp.where(rowmask, -jnp.exp(a_log) * jax.nn.softplus(bg[:, LANE:] + dt_bias), 0.0)
    z = z_ref[...]
    r = heads * sub
    row = _iota2((r, r), 0)
    col = _iota2((r, r), 1)
    same = (row // sub) == (col // sub)
    rows = _iota2((sub, 1), 0)
    nsteps = max(1, (sub - 1).bit_length())
    l2n = lambda a: a * lax.rsqrt(jnp.sum(a * a, axis=-1, keepdims=True) + RMS_EPS)
    for j in range(t // sub):
        sl = slice(j * sub, (j + 1) * sub)
        qkv_j, beta_j, mask_j, z_j = qkv[sl], beta[sl], rowmask[sl], z[sl]
        gc = g[sl]
        sh = 1
        while sh < sub:
            gc = gc + jnp.where(rows >= sh, pltpu.roll(gc, sh, 0), 0.0)
            sh *= 2
        stack = lambda f: jnp.concatenate([f(h) for h in range(heads)], axis=0)
        q = stack(lambda h: l2n(qkv_j[:, h * dk:(h + 1) * dk]) * (dk ** -0.5))
        k = stack(lambda h: l2n(qkv_j[:, nqk + h * dk:nqk + (h + 1) * dk]))
        v = stack(lambda h: qkv_j[:, 2 * nqk + h * dk:2 * nqk + (h + 1) * dk])
        beta_c = stack(lambda h: beta_j[:, h:h + 1])
        gc_c = stack(lambda h: gc[:, h:h + 1])
        gc_r = jnp.sum(jnp.where(row == col, gc_c, 0.0), axis=0, keepdims=True)
        decay = jnp.where(same & (row >= col), jnp.exp(gc_c - gc_r), 0.0)
        eg = jnp.exp(gc_c)
        kb = k * beta_c
        kc = k.astype(cdt)
        a_mat = jnp.where(same & (row > col), _dot_nt(kb.astype(cdt), kc) * decay, 0.0)
        y = jnp.concatenate([v * beta_c, kb * eg], axis=-1)
        p = -a_mat
        for i in range(nsteps):
            pc = p.astype(cdt)
            y = y + _dot(pc, y.astype(cdt))
            if i + 1 < nsteps:
                p = _dot(pc, pc)
        u = y[:, :dk]
        w = y[:, dk:]
        qe = q * eg
        ws_qs = [_dot(jnp.concatenate([w[h * sub:(h + 1) * sub], qe[h * sub:(h + 1) * sub]], axis=0).astype(cdt),
                      s_scr[h].astype(cdt)) for h in range(heads)]
        v_new = u - jnp.concatenate([a[:sub] for a in ws_qs], axis=0)
        attn = _dot_nt(q.astype(cdt), kc) * decay
        o = jnp.concatenate([a[sub:] for a in ws_qs], axis=0) + _dot(attn.astype(cdt), v_new.astype(cdt))
        o = o * lax.rsqrt(jnp.mean(o * o, axis=-1, keepdims=True) + RMS_EPS) * norm_g
        for h in range(heads):
            g_last = gc[sub - 1:sub, h:h + 1]
            k_dec = (k[h * sub:(h + 1) * sub] * jnp.exp(g_last - gc[:, h:h + 1])).astype(cdt)
            s_scr[h] = s_scr[h] * jnp.exp(g_last) + _dot_tn(k_dec, v_new[h * sub:(h + 1) * sub].astype(cdt))
            mixed_ref[sl, h * dk:(h + 1) * dk] = jnp.where(
                mask_j, o[h * sub:(h + 1) * sub] * _silu(z_j[:, h * dk:(h + 1) * dk]), 0.0)

    xb = jnp.where(rowmask, xb_ref[...], 0.0)
    xc = _causal_conv(xb_ext, xb, lw_ref[...], t) + lv_ref[0:1, :]

    @pl.when(c == c_last)
    def _():
        _put_state(lconv_out, ow, xb_ext[pl.ds(EXT0 + v_last - HIST, HIST), :])

    xb_ext[pl.ds(EXT0 - HIST, HIST), :] = xb_ext[pl.ds(EXT0 + t - HIST, HIST), :]
    xcb = xc.astype(BF16)
    r = jax.nn.sigmoid(_dot(xcb, wa_ref[...]) + lv_ref[1:2, :])
    gi = jax.nn.sigmoid(_dot(xcb, wx_ref[...]) + lv_ref[2:3, :])
    log_a = jnp.where(rowmask, -LRU_C * r * jax.nn.softplus(-lv_ref[3:4, :]), 0.0)
    a = jnp.exp(log_a)
    uu = jnp.where(rowmask, jnp.sqrt(1.0 - a * a) * (gi * xc), 0.0)
    rows = _iota2((t, 1), 0)
    sh = 1
    while sh < t:
        keep = rows >= sh
        uu = jnp.where(keep, a * pltpu.roll(uu, sh, 0) + uu, uu)
        a = jnp.where(keep, a * pltpu.roll(a, sh, 0), a)
        sh *= 2
    hh = uu + a * h_scr[...]
    h_scr[...] = hh[t - 1:t, :]
    yb = yb_ref[...]
    gelu = 0.5 * yb * (1.0 + jnp.tanh(math.sqrt(2.0 / math.pi) * (yb + 0.044715 * (yb * yb * yb))))
    mixed_ref[:, nqk:] = jnp.where(rowmask, gelu * hh, 0.0)

    @pl.when(c == c_last)
    def _():
        _put_state(s_out, ow, s_scr[...])
        _put_state(h_out, ow, h_scr[...])


def _state_out_specs(shapes, prev, lo, n_l, batch):
    lead = n_l if prev is None else 1
    first = 0 if prev is None else lo
    specs = tuple(pl.BlockSpec((lead, 1) + sh, functools.partial(lambda b, c, nz: (first, b) + (0,) * nz, nz=len(sh)))
                  for sh in shapes)
    out_shape = tuple(jax.ShapeDtypeStruct((n_l, batch) + sh, F32) for sh in shapes)
    return specs, out_shape, (lo if prev is None else 0)


def _ab_mixer(proj, dconv0, s0, lconv0, h0, wts, prev, *, t, sub, length, batch, cpb, rb0, li, lo, n_l):
    dnw, dnv, lw, lv, wa, wx = wts
    heads = s0.shape[2]
    nqk = heads * LANE
    nv = heads * s0.shape[4]
    lw_ch = lw.shape[1]
    dmix = nv + lw_ch
    rmap = lambda j: (lambda b, c: (rb0 + b * cpb + c, j))
    full2 = lambda b, c: (0, 0)
    in_specs = [
        pl.BlockSpec((t, 3 * nqk), rmap(0)),
        pl.BlockSpec((t, nv), rmap(3 * nqk // nv)),
        pl.BlockSpec((t, lw_ch), rmap((3 * nqk + nv) // lw_ch)),
        pl.BlockSpec((t, lw_ch), rmap((3 * nqk + nv) // lw_ch + 1)),
        pl.BlockSpec((t, 2 * LANE), rmap((3 * nqk + nv + 2 * lw_ch) // (2 * LANE))),
        pl.BlockSpec((1, 1, HIST, 3 * nqk), lambda b, c: (li, b, 0, 0)),
        pl.BlockSpec((1, 1) + s0.shape[2:], lambda b, c: (li, b, 0, 0, 0)),
        pl.BlockSpec((1, 1, HIST, lw_ch), lambda b, c: (li, b, 0, 0)),
        pl.BlockSpec((1, 1, lw_ch), lambda b, c: (b, 0, 0)),
        pl.BlockSpec(dnw.shape, full2), pl.BlockSpec(dnv.shape, full2),
        pl.BlockSpec(lw.shape, full2), pl.BlockSpec(lv.shape, full2),
        pl.BlockSpec(wa.shape, full2), pl.BlockSpec(wx.shape, full2),
    ]
    args = [proj, proj, proj, proj, proj, dconv0, s0, lconv0, h0, dnw, dnv, lw, lv, wa, wx]
    st_specs, st_shapes, ow = _state_out_specs(
        (s0.shape[2:], (HIST, 3 * nqk), (1, lw_ch), (HIST, lw_ch)), prev, lo, n_l, batch)
    aliases = {}
    if prev is not None:
        in_specs += [pl.BlockSpec(memory_space=pl.ANY)] * len(prev)
        aliases = {len(args) + j: 1 + j for j in range(len(prev))}
        args += list(prev)
    return pl.pallas_call(
        functools.partial(_ab_kernel, t=t, sub=sub, length=length, heads=heads, ow=ow),
        grid=(batch, cpb),
        in_specs=in_specs,
        out_specs=(pl.BlockSpec((t, dmix), lambda b, c: (b * cpb + c, 0)),) + st_specs,
        out_shape=(jax.ShapeDtypeStruct((batch * cpb * t, dmix), F32),) + st_shapes,
        scratch_shapes=[pltpu.VMEM((t + EXT0, 3 * nqk), F32), pltpu.VMEM((t + EXT0, lw_ch), F32),
                        pltpu.VMEM(s0.shape[2:], F32), pltpu.VMEM((1, lw_ch), F32)],
        input_output_aliases=aliases,
        compiler_params=pltpu.CompilerParams(dimension_semantics=("parallel", "arbitrary"),
                                             vmem_limit_bytes=VMEM_LIMIT),
        name="ab_mixer")(*args)


def _ssd_kernel(z_ref, xs_ref, bc_ref, dt_ref, xconv0_ref, bconv0_ref, s0_ref,
                wxs_ref, wbc_ref, bxs_ref, bbc_ref, vec_ref, ng_ref, *rest,
                t, length, heads, groups, ow):
    mixed_ref, s_out, conv_out, xs_ext, bc_ext, s_scr, ybuf = rest[-7:]
    c = pl.program_id(1)
    c_last = (length - 1) // t
    v_last = length - c_last * t
    inner = xs_ref.shape[1]
    p = inner // heads
    n = s_scr.shape[2]
    hpg = heads // groups
    gw = inner // groups
    cdt = BF16 if t % 16 == 0 else F32

    @pl.when(c == 0)
    def _():
        xs_ext[pl.ds(0, EXT0), :] = jnp.zeros((EXT0, inner), F32)
        bc_ext[pl.ds(0, EXT0), :] = jnp.zeros((EXT0, inner), F32)
        xs_ext[pl.ds(EXT0 - HIST, HIST), :] = xconv0_ref[0, 0]
        bc_ext[pl.ds(EXT0 - HIST, HIST), :] = bconv0_ref[0, 0]
        s_scr[...] = s0_ref[0, 0]

    rowmask = _iota2((t, 1), 0) < (length - c * t)
    xs = _silu(_causal_conv(xs_ext, jnp.where(rowmask, xs_ref[...], 0.0), wxs_ref[...], t) + bxs_ref[...])
    bc = _silu(_causal_conv(bc_ext, jnp.where(rowmask, bc_ref[...], 0.0), wbc_ref[...], t) + bbc_ref[...])

    @pl.when(c == c_last)
    def _():
        _put_state(conv_out, ow, xs_ext[pl.ds(EXT0 + v_last - HIST, HIST), :], (slice(None), slice(0, inner)))
        _put_state(conv_out, ow, bc_ext[pl.ds(EXT0 + v_last - HIST, HIST), :], (slice(None), slice(inner, 2 * inner)))

    xs_ext[pl.ds(EXT0 - HIST, HIST), :] = xs_ext[pl.ds(EXT0 + t - HIST, HIST), :]
    bc_ext[pl.ds(EXT0 - HIST, HIST), :] = bc_ext[pl.ds(EXT0 + t - HIST, HIST), :]

    dt = jnp.where(rowmask, jax.nn.softplus(jnp.where(rowmask, dt_ref[...], 0.0) + vec_ref[0:1, :]), 0.0)
    da = dt * (-jnp.exp(vec_ref[1:2, :]))
    d_skip = vec_ref[2:3, :]
    cs, cst = _cumsum_rows(da)
    ecs = jnp.exp(cs)
    e_last = jnp.exp(cs[t - 1:t, :])
    w_dec = jnp.exp(cs[t - 1:t, :] - cs)
    causal = _iota2((t, t), 0) >= _iota2((t, t), 1)
    gn = groups * n
    lane_g = _iota2((t, gw), 1)
    row_g = _iota2((gw, 1), 0)
    for g in range(groups):
        e0 = g * hpg
        bm = bc[:, g * n:(g + 1) * n].astype(cdt)
        cm = bc[:, gn + g * n:gn + (g + 1) * n].astype(cdt)
        cb = _dot_nt(cm, bm)
        xg = xs[:, g * gw:(g + 1) * gw]
        xdt = xg * _expand_heads(dt, e0, hpg, p)
        m_cat = jnp.concatenate(
            [(cb * jnp.where(causal, jnp.exp(cs[:, e0 + j:e0 + j + 1] - cst[e0 + j:e0 + j + 1, :]), 0.0)).astype(cdt)
             for j in range(hpg)], axis=1)
        x_bd = jnp.concatenate(
            [jnp.where((lane_g >= j * p) & (lane_g < (j + 1) * p), xdt, 0.0) for j in range(hpg)], axis=0).astype(cdt)
        sg = s_scr[pl.ds(e0, hpg)].reshape(gw, n)
        y = _dot(m_cat, x_bd) + _dot_nt(cm, sg.astype(cdt)) * _expand_heads(ecs, e0, hpg, p)
        el = e_last[:, e0:e0 + 1]
        for j in range(1, hpg):
            el = jnp.where(row_g >= j * p, e_last[:, e0 + j:e0 + j + 1], el)
        s_new = sg * el + _dot_tn((xdt * _expand_heads(w_dec, e0, hpg, p)).astype(cdt), bm)
        s_scr[pl.ds(e0, hpg)] = s_new.reshape(hpg, p, n)
        ybuf[:, g * gw:(g + 1) * gw] = y + xg * _expand_heads(d_skip, e0, hpg, p)
    yz = ybuf[...] * _silu(z_ref[...])
    ng = ng_ref[...]
    for g in range(groups):
        seg = yz[:, g * gw:(g + 1) * gw]
        seg = seg * lax.rsqrt(jnp.mean(seg * seg, axis=-1, keepdims=True) + RMS_EPS) * ng[:, g * gw:(g + 1) * gw]
        mixed_ref[:, g * gw:(g + 1) * gw] = jnp.where(rowmask, seg, 0.0)

    @pl.when(c == c_last)
    def _():
        _put_state(s_out, ow, s_scr[...])


def _ssd_mixer(proj, conv0, s0, wts, prev, *, t, length, batch, cpb, rb0, groups, li, lo, n_l):
    cw, cb, vec, ng = wts
    heads = s0.shape[2]
    inner = heads * s0.shape[3]
    rmap = lambda j: (lambda b, c: (rb0 + b * cpb + c, j))
    in_specs = [
        pl.BlockSpec((t, inner), rmap(0)), pl.BlockSpec((t, inner), rmap(1)), pl.BlockSpec((t, inner), rmap(2)),
        pl.BlockSpec((t, LANE), rmap(3 * inner // LANE)),
        pl.BlockSpec((1, 1, HIST, inner), lambda b, c: (li, b, 0, 0)),
        pl.BlockSpec((1, 1, HIST, inner), lambda b, c: (li, b, 0, 1)),
        pl.BlockSpec((1, 1) + s0.shape[2:], lambda b, c: (li, b, 0, 0, 0)),
        pl.BlockSpec((CONV_W, inner), lambda b, c: (0, 0)), pl.BlockSpec((CONV_W, inner), lambda b, c: (0, 1)),
        pl.BlockSpec((1, inner), lambda b, c: (0, 0)), pl.BlockSpec((1, inner), lambda b, c: (0, 1)),
        pl.BlockSpec(vec.shape, lambda b, c: (0, 0)), pl.BlockSpec(ng.shape, lambda b, c: (0, 0)),
    ]
    args = [proj, proj, proj, proj, conv0, conv0, s0, cw, cw, cb, cb, vec, ng]
    st_specs, st_shapes, ow = _state_out_specs((s0.shape[2:], (HIST, conv0.shape[3])), prev, lo, n_l, batch)
    aliases = {}
    if prev is not None:
        in_specs += [pl.BlockSpec(memory_space=pl.ANY)] * len(prev)
        aliases = {len(args) + j: 1 + j for j in range(len(prev))}
        args += list(prev)
    return pl.pallas_call(
        functools.partial(_ssd_kernel, t=t, length=length, heads=heads, groups=groups, ow=ow),
        grid=(batch, cpb),
        in_specs=in_specs,
        out_specs=(pl.BlockSpec((t, inner), lambda b, c: (b * cpb + c, 0)),) + st_specs,
        out_shape=(jax.ShapeDtypeStruct((batch * cpb * t, inner), F32),) + st_shapes,
        scratch_shapes=[pltpu.VMEM((t + EXT0, inner), F32), pltpu.VMEM((t + EXT0, inner), F32),
                        pltpu.VMEM(s0.shape[2:], F32), pltpu.VMEM((t, inner), F32)],
        input_output_aliases=aliases,
        compiler_params=pltpu.CompilerParams(dimension_semantics=("parallel", "arbitrary"),
                                             vmem_limit_bytes=VMEM_LIMIT),
        name="ssd_mixer")(*args)


def _layer_norm(h, g, b):
    mu = jnp.mean(h, axis=-1, keepdims=True)
    d = h - mu
    var = jnp.mean(d * d, axis=-1, keepdims=True)
    return d * lax.rsqrt(var + LN_EPS) * g + b


def _col_to_rows(v):
    eye = _iota2((LANE, LANE), 0) == _iota2((LANE, LANE), 1)
    rows = [jnp.sum(jnp.where(eye, v[i * LANE:(i + 1) * LANE, :], 0.0), axis=0, keepdims=True)
            for i in range(v.shape[0] // LANE)]
    return jnp.concatenate(rows, axis=0)


def _outln_kernel(yp_ref, ys_ref, x_ref, w_ref, ln_ref, wr_ref, br_ref, o_ref, bkt_ref, rank_ref, cnt_ref,
                  ybuf, cnt_scr, *, alpha, n_groups, epg, nbp, n_buckets):
    d = x_ref.shape[1]
    i = pl.program_id(0)

    @pl.when(i == 0)
    def _():
        cnt_scr[...] = jnp.zeros_like(cnt_scr)

    @pl.when(i < nbp)
    def _():
        ybuf[...] = yp_ref[...].astype(BF16)

    @pl.when(i >= nbp)
    def _():
        ybuf[...] = ys_ref[...].astype(BF16)

    acc = _dot(ybuf[...], w_ref[...])
    xn = _layer_norm(alpha * x_ref[...] + acc, ln_ref[0:1, :], ln_ref[1:2, :])
    o_ref[:, :d] = xn
    xh = xn.astype(BF16)
    xl = (xn - xh.astype(F32)).astype(BF16)
    hw = _dot(xh, wr_ref[...])
    logits = hw[:, :ROUTE_LANES] + hw[:, ROUTE_LANES:] + _dot(xl, wr_ref[:, :ROUTE_LANES]) + br_ref[...]
    tm = logits.shape[0]
    lane = _iota2((tm, ROUTE_LANES), 1)
    ninf = -jnp.inf
    gl = jnp.where(lane < n_groups, logits, ninf)
    gmax = jnp.max(gl, axis=-1, keepdims=True)
    gidx = jnp.min(jnp.where(gl == gmax, lane, ROUTE_LANES), axis=-1, keepdims=True)
    g_w = 1.0 / jnp.sum(jnp.where(lane < n_groups, jnp.exp(logits - gmax), 0.0), axis=-1, keepdims=True)
    lo = n_groups + epg * gidx
    el = jnp.where((lane >= lo) & (lane < lo + epg), logits, ninf)
    m1 = jnp.max(el, axis=-1, keepdims=True)
    i1 = jnp.min(jnp.where(el == m1, lane, ROUTE_LANES), axis=-1, keepdims=True)
    el2 = jnp.where(lane == i1, ninf, el)
    m2 = jnp.max(el2, axis=-1, keepdims=True)
    i2 = jnp.min(jnp.where(el2 == m2, lane, ROUTE_LANES), axis=-1, keepdims=True)
    r = jnp.exp(m2 - m1)
    w1 = g_w / (1.0 + r)
    w2 = g_w * r / (1.0 + r)
    e1 = i1 - lo
    e2 = i2 - lo
    first = e1 < e2
    ea = jnp.where(first, e1, e2)
    eb = jnp.where(first, e2, e1)
    bucket = gidx * PAIR_SLOTS + ea * epg + eb
    ga = jnp.where(first, w1, w2)
    gb = jnp.where(first, w2, w1)
    o_ref[:, d:] = jnp.where(lane == 1, ga, jnp.where(lane == 2, gb, 0.0))
    onehot = _iota2((tm, n_buckets), 1) == bucket
    oh = jnp.where(onehot, 1.0, 0.0)
    earlier = jnp.where(_iota2((tm, tm), 0) > _iota2((tm, tm), 1), 1.0, 0.0).astype(BF16)
    before = _dot(earlier, oh.astype(BF16)) + cnt_scr[...]
    rank = jnp.sum(jnp.where(onehot, before, 0.0), axis=-1, keepdims=True)
    cnt_scr[...] = cnt_scr[...] + jnp.sum(oh, axis=0, keepdims=True)
    bkt_ref[0] = _col_to_rows(bucket.astype(F32)).astype(jnp.int32)
    rank_ref[0] = _col_to_rows(rank).astype(jnp.int32)
    cnt_ref[...] = jnp.broadcast_to(cnt_scr[...], cnt_ref.shape).astype(jnp.int32)


def _outln(mixed_p, mixed_s, x, w, ln, wr, br, *, n, tm, alpha, n_groups, epg, n_buckets):
    d = x.shape[1]
    dm = mixed_p.shape[1]
    nbp = mixed_p.shape[0] // tm
    nb = n // tm
    return pl.pallas_call(
        functools.partial(_outln_kernel, alpha=alpha, n_groups=n_groups, epg=epg, nbp=nbp, n_buckets=n_buckets),
        grid=(nb,),
        in_specs=[pl.BlockSpec((tm, dm), lambda i: (jnp.minimum(i, nbp - 1), 0)),
                  pl.BlockSpec((tm, dm), lambda i: (jnp.maximum(i - nbp, 0), 0)),
                  pl.BlockSpec((tm, d), lambda i: (i, 0)),
                  pl.BlockSpec((dm, d), lambda i: (0, 0), pipeline_mode=pl.Buffered(1)),
                  pl.BlockSpec(ln.shape, lambda i: (0, 0)),
                  pl.BlockSpec(wr.shape, lambda i: (0, 0)),
                  pl.BlockSpec(br.shape, lambda i: (0, 0))],
        out_specs=(pl.BlockSpec((tm, d + ROUTE_LANES), lambda i: (i, 0)),
                   pl.BlockSpec((1, tm // LANE, LANE), lambda i: (i, 0, 0)),
                   pl.BlockSpec((1, tm // LANE, LANE), lambda i: (i, 0, 0)),
                   pl.BlockSpec((8, n_buckets), lambda i: (0, 0))),
        out_shape=(jax.ShapeDtypeStruct((n, d + ROUTE_LANES), F32),
                   jax.ShapeDtypeStruct((nb, tm // LANE, LANE), jnp.int32),
                   jax.ShapeDtypeStruct((nb, tm // LANE, LANE), jnp.int32),
                   jax.ShapeDtypeStruct((8, n_buckets), jnp.int32)),
        scratch_shapes=[pltpu.VMEM((tm, dm), BF16), pltpu.VMEM((1, n_buckets), F32)],
        compiler_params=pltpu.CompilerParams(dimension_semantics=("arbitrary",), vmem_limit_bytes=VMEM_LIMIT),
        name="outproj_ln_route")(mixed_p, mixed_s, x, w, ln, wr, br)


def _moe_kernel(c8_ref, ea_ref, eb_ref, src_ref, nx1_ref, nx2_ref, x_hbm, w1a_ref, w3a_ref, w2a_ref, w1b_ref, w3b_ref,
                w2b_ref, ln_ref, o_hbm, xbuf, obuf, gsem, ssem, *, alpha, tr, n_rows):
    t = pl.program_id(0)
    nt = pl.num_programs(0)
    d = obuf.shape[2]
    slot = lax.rem(t, 2)
    other = 1 - slot
    nbuf = GATHER_AHEAD + 1
    gslot = lax.rem(t, nbuf)
    c8_at = lambda k: jnp.where(t + k < nt, c8_ref[jnp.minimum(t + k, nt - 1)], 0)
    c8_t = c8_ref[t]
    c8_next = c8_at(1)

    def gather(idx_ref, s, c8):
        for g in range(tr // DMA_ROWS):
            @pl.when(g < c8)
            def _():
                for u in range(DMA_ROWS):
                    r = g * DMA_ROWS + u
                    pltpu.make_async_copy(x_hbm.at[pl.ds(jnp.minimum(idx_ref[0, 0, r], n_rows - 1), 1), :],
                                          xbuf.at[s, pl.ds(r, 1), :], gsem.at[s]).start(priority=u % 2)

    def scatter(s, c8):
        for g in range(tr // DMA_ROWS):
            @pl.when(g < c8)
            def _():
                for u in range(DMA_ROWS):
                    r = g * DMA_ROWS + u
                    pltpu.make_async_copy(obuf.at[s, pl.ds(r, 1), :],
                                          o_hbm.at[pl.ds(src_ref[0, 0, r], 1), :], ssem.at[s]).start(priority=u % 2)

    def wait_groups(desc, c8):
        def body(i, carry):
            desc.wait()
            return carry
        lax.fori_loop(0, c8, body, 0)

    def wait_gather(s, c8):
        wait_groups(pltpu.make_async_copy(x_hbm.at[pl.ds(0, DMA_ROWS), :], xbuf.at[s, pl.ds(0, DMA_ROWS), :],
                                          gsem.at[s]), c8)

    def wait_scatter(s, c8):
        wait_groups(pltpu.make_async_copy(obuf.at[s, pl.ds(0, DMA_ROWS), :], o_hbm.at[pl.ds(0, DMA_ROWS), :],
                                          ssem.at[s]), c8)

    @pl.when(t == 0)
    def _():
        xbuf[...] = jnp.zeros_like(xbuf)
        obuf[...] = jnp.zeros_like(obuf)
        for s in range(2):
            fill = pltpu.make_async_copy(obuf.at[s], o_hbm.at[pl.ds(n_rows + s * tr, tr), :], ssem.at[s])
            fill.start()
            fill.wait()
        gather(src_ref, 0, c8_t)
        gather(nx1_ref, 1, c8_next)

    gather(nx2_ref, lax.rem(t + GATHER_AHEAD, nbuf), c8_at(GATHER_AHEAD))

    @pl.when(c8_t > 0)
    def _():
        wait_gather(gslot, c8_t)
        x = xbuf[gslot, :, :d]
        ga = xbuf[gslot, :, d + 1:d + 2]
        gb = xbuf[gslot, :, d + 2:d + 3]
        xb = x.astype(BF16)
        ha = _silu(_dot(xb, w1a_ref[0])) * _dot(xb, w3a_ref[0])
        hb = _silu(_dot(xb, w1b_ref[0])) * _dot(xb, w3b_ref[0])
        ffn = _dot((ha * ga).astype(BF16), w2a_ref[0]) + _dot((hb * gb).astype(BF16), w2b_ref[0])
        y = _layer_norm(alpha * x + ffn, ln_ref[0:1, :], ln_ref[1:2, :])

        @pl.when(t >= 2)
        def _():
            wait_scatter(slot, c8_ref[jnp.maximum(t - 2, 0)])

        obuf[slot] = y
        scatter(slot, c8_t)

        @pl.when(c8_next == 0)
        def _():
            wait_scatter(slot, c8_t)

            @pl.when(t >= 1)
            def _():
                wait_scatter(other, c8_ref[jnp.maximum(t - 1, 0)])


def _moe(x1r, src, c8, ea, eb, w1, w3, w2, ln, *, alpha, tr):
    n, dr = x1r.shape
    d = dr - ROUTE_LANES
    nt = c8.shape[0]
    f = w1.shape[2]
    wa = lambda t, v, a, b: (a[t], 0, 0)
    wb = lambda t, v, a, b: (b[t], 0, 0)
    grid_spec = pltpu.PrefetchScalarGridSpec(
        num_scalar_prefetch=3,
        grid=(nt,),
        in_specs=[pl.BlockSpec((1, 1, tr), lambda t, v, a, b: (t, 0, 0), memory_space=pltpu.SMEM),
                  pl.BlockSpec((1, 1, tr), lambda t, v, a, b: (jnp.minimum(t + 1, nt - 1), 0, 0),
                               memory_space=pltpu.SMEM),
                  pl.BlockSpec((1, 1, tr), lambda t, v, a, b: (jnp.minimum(t + GATHER_AHEAD, nt - 1), 0, 0),
                               memory_space=pltpu.SMEM),
                  pl.BlockSpec(memory_space=pl.ANY),
                  pl.BlockSpec((1, d, f), wa), pl.BlockSpec((1, d, f), wa), pl.BlockSpec((1, f, d), wa),
                  pl.BlockSpec((1, d, f), wb), pl.BlockSpec((1, d, f), wb), pl.BlockSpec((1, f, d), wb),
                  pl.BlockSpec(ln.shape, lambda t, v, a, b: (0, 0))],
        out_specs=pl.BlockSpec(memory_space=pl.ANY),
        scratch_shapes=[pltpu.VMEM((GATHER_AHEAD + 1, tr, dr), F32), pltpu.VMEM((2, tr, d), F32),
                        pltpu.SemaphoreType.DMA((GATHER_AHEAD + 1,)), pltpu.SemaphoreType.DMA((2,))])
    return pl.pallas_call(
        functools.partial(_moe_kernel, alpha=alpha, tr=tr, n_rows=n),
        grid_spec=grid_spec,
        out_shape=jax.ShapeDtypeStruct((n + 2 * tr, d), F32),
        compiler_params=pltpu.CompilerParams(dimension_semantics=("arbitrary",), vmem_limit_bytes=VMEM_LIMIT),
        name="moe_routed")(c8, ea, eb, src, src, src, x1r, w1, w3, w2, w1, w3, w2, ln)


def _route_schedule(bucket, rank, counts, *, n, tr, nt, epg, e_off):
    nbk = counts.shape[0]
    tiles_b = (counts + tr - 1) // tr
    tile_end = jnp.cumsum(tiles_b)
    pstart = (tile_end - tiles_b) * tr
    hit = bucket[:, None] == jnp.arange(nbk, dtype=jnp.int32)[None, :]
    dest = jnp.sum(jnp.where(hit, pstart[None, :], 0), axis=1) + rank
    slots = jnp.arange(nt * tr, dtype=jnp.int32)
    trash = n + ((slots // tr) % 2) * tr + slots % tr
    src = trash.at[dest].set(jnp.arange(n, dtype=jnp.int32))
    tiles = jnp.arange(nt, dtype=jnp.int32)
    tb = jnp.sum(tile_end[None, :] <= jnp.minimum(tiles, tile_end[-1] - 1)[:, None], axis=1).astype(jnp.int32)
    own = tb[:, None] == jnp.arange(nbk, dtype=jnp.int32)[None, :]
    pick = lambda tab: jnp.sum(jnp.where(own, tab[None, :], 0), axis=1)
    rows_t = jnp.clip(pick(counts) - (tiles - pick(tile_end - tiles_b)) * tr, 0, tr)
    c8 = jnp.where(tiles < tile_end[-1], (rows_t + DMA_ROWS - 1) // DMA_ROWS, 0)
    grp = tb // PAIR_SLOTS
    pair = tb % PAIR_SLOTS
    ea = e_off + grp * epg + pair // epg
    eb = e_off + grp * epg + pair % epg
    return src.reshape(nt, 1, tr), c8.astype(jnp.int32), ea.astype(jnp.int32), eb.astype(jnp.int32)


def _pad_cols(w, width):
    return jnp.pad(w, ((0, 0), (0, width - w.shape[1])))


def _pad_vec(v, width=LANE):
    return jnp.pad(v.astype(F32), (0, width - v.shape[0]))[None, :]


def _rows8(*vecs):
    m = jnp.stack([v.astype(F32) for v in vecs], axis=0)
    return jnp.pad(m, ((0, 8 - m.shape[0]), (0, 0)))


def _block_diag(w):
    nb, bd, _ = w.shape
    eye = jnp.eye(nb, dtype=w.dtype)
    return jnp.einsum("nde,nm->ndme", w, eye).reshape(nb * bd, nb * bd)


def kernel(x_prompt, x_sample, state_delta, state_delta_conv, state_lru, state_lru_conv, state_ssm, state_ssm_conv,
           meta, w_in_ab, dn_conv_w, dn_a_log, dn_dt_bias, dn_norm_g, lru_conv_w, lru_conv_b, lru_w_a, lru_b_a,
           lru_w_x, lru_b_x, lru_lam, w_out_ab, w_in_ssd, ssd_conv_w, ssd_conv_b, ssd_dt_bias, ssd_a_log, ssd_d,
           ssd_norm_g, w_out_ssd, ln_g, ln_b, moe_w_group, moe_b_group, moe_w_expert, moe_b_expert, moe_w1, moe_w3,
           moe_w2):
    bp, seq, d = x_prompt.shape
    bs, ls, _ = x_sample.shape
    depth = ln_g.shape[0]
    alpha = (2.0 * depth) ** 0.25
    n_groups, epg = moe_w1.shape[1], moe_w1.shape[2]
    heads_dn = dn_a_log.shape[1]
    nqk = heads_dn * state_delta.shape[3]
    nv = heads_dn * state_delta.shape[4]
    lru_w = lru_lam.shape[1]
    heads_ssd = ssd_a_log.shape[1]
    inner = heads_ssd * state_ssm.shape[3]
    n_state = state_ssm.shape[4]
    ssd_groups = (ssd_conv_w.shape[2] - inner) // (2 * n_state)
    assert nqk == nv and state_delta.shape[3] == LANE and state_delta.shape[4] == LANE and n_state == LANE

    lp_len = N_META + seq
    chunk_lcm = math.lcm(DN_BLOCK, SSD_CHUNK)
    lpad = -(-lp_len // chunk_lcm) * chunk_lcm
    np_rows = bp * lpad
    ns_rows = bs * ls
    ntot = np_rows + ns_rows
    tm = math.gcd(math.gcd(np_rows, ns_rows), ROW_TILE)
    assert tm % LANE == 0 and np_rows % ls == 0 and ls % 8 == 0 and ls >= HIST
    assert (lp_len - 1) % DN_BLOCK + 1 >= HIST and (lp_len - 1) % SSD_CHUNK + 1 >= HIST

    xp = jnp.concatenate([jnp.broadcast_to(meta.astype(F32), (bp, N_META, d)), x_prompt,
                          jnp.zeros((bp, lpad - lp_len, d), F32)], axis=1)
    x = jnp.concatenate([xp.reshape(np_rows, d), x_sample.reshape(ns_rows, d)], axis=0)

    tr = MOE_TILE
    n_buckets = n_groups * PAIR_SLOTS
    n_pairs = n_groups * (epg * (epg - 1) // 2)
    nt = -(-(ntot + n_pairs * (tr - 1)) // tr)
    n_exp = n_groups * epg
    w1 = moe_w1.reshape(depth * n_exp, d, -1).astype(BF16)
    w3 = moe_w3.reshape(depth * n_exp, d, -1).astype(BF16)
    w2 = moe_w2.reshape(depth * n_exp, -1, d).astype(BF16)

    zeros = lambda *s: jnp.zeros(s, F32)
    n_ab, n_ssd = (depth + 1) // 2, depth // 2
    st_ab_p = st_ab_s = st_ssd_p = st_ssd_s = None
    for layer in range(depth):
        i = layer // 2
        if layer % 2 == 0:
            wi = w_in_ab[i]
            o_b = 3 * nqk + nv
            w_in = jnp.concatenate([wi[:, :o_b], wi[:, o_b + 2 * heads_dn:],
                                    _pad_cols(wi[:, o_b:o_b + heads_dn], LANE),
                                    _pad_cols(wi[:, o_b + heads_dn:o_b + 2 * heads_dn], LANE)], axis=1).astype(BF16)
            proj = _inproj(x, w_in, tm, ntot)
            wts = (dn_conv_w[i], _rows8(_pad_vec(dn_a_log[i])[0], _pad_vec(dn_dt_bias[i])[0], dn_norm_g[i]),
                   lru_conv_w[i], _rows8(lru_conv_b[i], lru_b_a[i], lru_b_x[i], lru_lam[i]),
                   _block_diag(lru_w_a[i]).astype(BF16), _block_diag(lru_w_x[i]).astype(BF16))
            mixed_p, *st_ab_p = _ab_mixer(
                proj, zeros(1, bp, HIST, 3 * nqk), zeros(1, bp, *state_delta.shape[2:]), zeros(1, bp, HIST, lru_w),
                zeros(bp, 1, lru_w), wts, st_ab_p, t=DN_BLOCK, sub=DN_CHUNK, length=lp_len, batch=bp,
                cpb=lpad // DN_BLOCK, rb0=0, li=0, lo=i, n_l=n_ab)
            mixed_s, *st_ab_s = _ab_mixer(
                proj, state_delta_conv, state_delta, state_lru_conv, state_lru[i][:, None, :], wts, st_ab_s,
                t=ls, sub=ls, length=ls, batch=bs, cpb=1, rb0=np_rows // ls, li=i, lo=i, n_l=n_ab)
            w_out = w_out_ab[i].astype(BF16)
        else:
            wi = w_in_ssd[i]
            e_raw = wi.shape[1]
            w_in = _pad_cols(wi, -(-e_raw // LANE) * LANE).astype(BF16)
            proj = _inproj(x, w_in, tm, ntot)
            wts = (ssd_conv_w[i], ssd_conv_b[i][None, :],
                   _rows8(_pad_vec(ssd_dt_bias[i])[0], _pad_vec(ssd_a_log[i])[0], _pad_vec(ssd_d[i])[0]),
                   ssd_norm_g[i][None, :])
            mixed_p, *st_ssd_p = _ssd_mixer(
                proj, zeros(1, bp, HIST, ssd_conv_w.shape[2]), zeros(1, bp, *state_ssm.shape[2:]), wts, st_ssd_p,
                t=SSD_CHUNK, length=lp_len, batch=bp, cpb=lpad // SSD_CHUNK, rb0=0, groups=ssd_groups,
                li=0, lo=i, n_l=n_ssd)
            mixed_s, *st_ssd_s = _ssd_mixer(
                proj, state_ssm_conv, state_ssm, wts, st_ssd_s,
                t=ls, length=ls, batch=bs, cpb=1, rb0=np_rows // ls, groups=ssd_groups, li=i, lo=i, n_l=n_ssd)
            w_out = w_out_ssd[i].astype(BF16)
        wr = _pad_cols(jnp.concatenate([moe_w_group[layer], moe_w_expert[layer]], axis=1), ROUTE_LANES)
        wr_hi = wr.astype(BF16)
        wr = jnp.concatenate([wr_hi, (wr - wr_hi.astype(F32)).astype(BF16)], axis=1)
        br = _pad_vec(jnp.concatenate([moe_b_group[layer], moe_b_expert[layer]]), ROUTE_LANES)
        x1r, bucket, rank, counts = _outln(mixed_p, mixed_s, x, w_out, _rows8(ln_g[layer, 0], ln_b[layer, 0]), wr, br,
                                           n=ntot, tm=tm, alpha=alpha, n_groups=n_groups, epg=epg,
                                           n_buckets=n_buckets)
        src, c8, ea, eb = _route_schedule(bucket.reshape(ntot), rank.reshape(ntot), counts[0], n=ntot, tr=tr,
                                             nt=nt, epg=epg, e_off=layer * n_exp)
        x = _moe(x1r, src, c8, ea, eb, w1, w3, w2, _rows8(ln_g[layer, 1], ln_b[layer, 1]), alpha=alpha, tr=tr)

    y_prompt = x[:np_rows].reshape(bp, lpad, d)[:, N_META:lp_len]
    y_sample = x[np_rows:ntot].reshape(bs, ls, d)
    ab = lambda st: (st[0], st[1], st[2][:, :, 0], st[3])
    return (y_prompt, y_sample) + ab(st_ab_p) + tuple(st_ssd_p) + ab(st_ab_s) + tuple(st_ssd_s)
```

```python
import functools
import math

import jax
import jax.numpy as jnp
from jax import lax
from jax.experimental import pallas as pl
from jax.experimental.pallas import tpu as pltpu

F32 = jnp.float32
BF16 = jnp.bfloat16
HI = lax.Precision.HIGHEST

LN_EPS = 1e-5
RMS_EPS = 1e-6
LRU_C = 8.0
N_META = 16
CONV_W = 4
HIST = CONV_W - 1

LANE = 128
EXT0 = 8
DN_CHUNK = 64
DN_BLOCK = 128
SSD_CHUNK = 128
ROW_TILE = 512
MOE_TILE = 256
DMA_ROWS = 8
PAIR_SLOTS = 64
ROUTE_LANES = 128
VMEM_LIMIT = 56 * 1024 * 1024


def _silu(x):
    return x * jax.nn.sigmoid(x)


def _dot(a, b, **kw):
    return jnp.dot(a, b, preferred_element_type=F32, **kw)


def _dot_nt(a, b, **kw):
    return lax.dot_general(a, b, (((1,), (1,)), ((), ())), preferred_element_type=F32, **kw)


def _dot_tn(a, b, **kw):
    return lax.dot_general(a, b, (((0,), (0,)), ((), ())), preferred_element_type=F32, **kw)


def _iota2(shape, dim):
    return lax.broadcasted_iota(jnp.int32, shape, dim)


def _cumsum_rows(x):
    t = x.shape[0]
    rows = _iota2((t, 1), 0)
    cs = x
    sh = 1
    while sh < t:
        cs = cs + jnp.where(rows >= sh, pltpu.roll(cs, sh, 0), 0.0)
        sh *= 2
    if t == LANE:
        cst = cs.T
    else:
        eye = (_iota2((LANE, LANE), 0) == _iota2((LANE, LANE), 1)).astype(F32)
        cst = _dot_nt(eye, cs, precision=HI)
    return cs, cst


def _expand_heads(v, e0, nh, p):
    lane = _iota2((v.shape[0], nh * p), 1)
    out = jnp.broadcast_to(v[:, e0:e0 + 1], (v.shape[0], nh * p))
    for j in range(1, nh):
        out = jnp.where(lane >= j * p, v[:, e0 + j:e0 + j + 1], out)
    return out


def _put_state(ref, ow, val, sl=()):
    ref[(ow, 0) + sl] = val
    for j in range(ref.shape[0]):
        if j != ow:
            ref[(j, 0) + sl] = jnp.zeros_like(val)


def _causal_conv(ext_ref, x, w, t):
    ext_ref[pl.ds(EXT0, t), :] = x
    ext = ext_ref[...]
    y = x * w[HIST:CONV_W, :]
    for k in range(HIST):
        y = y + pltpu.roll(ext, HIST - k, 0)[EXT0:EXT0 + t, :] * w[k:k + 1, :]
    return y


def _inproj_kernel(x_ref, w_ref, o_ref, *, tn):
    xb = x_ref[...].astype(BF16)
    for j in range(o_ref.shape[1] // tn):
        o_ref[:, j * tn:(j + 1) * tn] = _dot(xb, w_ref[:, j * tn:(j + 1) * tn])


def _inproj(x, w, tm, n):
    k = x.shape[1]
    e = w.shape[1]
    tn = max(c for c in range(LANE, 1024 + LANE, LANE) if e % c == 0)
    return pl.pallas_call(
        functools.partial(_inproj_kernel, tn=tn),
        grid=(n // tm,),
        in_specs=[pl.BlockSpec((tm, k), lambda i: (i, 0)),
                  pl.BlockSpec((k, e), lambda i: (0, 0), pipeline_mode=pl.Buffered(1))],
        out_specs=pl.BlockSpec((tm, e), lambda i: (i, 0)),
        out_shape=jax.ShapeDtypeStruct((n, e), F32),
        compiler_params=pltpu.CompilerParams(dimension_semantics=("parallel",), vmem_limit_bytes=VMEM_LIMIT),
        name="inproj")(x, w)


def _ab_kernel(qkv_ref, z_ref, xb_ref, yb_ref, bg_ref, dconv0_ref, s0_ref, lconv0_ref, h0_ref,
               dnw_ref, dnv_ref, lw_ref, lv_ref, wa_ref, wx_ref, *rest,
               t, sub, length, heads, ow):
    mixed_ref, s_out, dconv_out, h_out, lconv_out, qkv_ext, xb_ext, s_scr, h_scr = rest[-9:]
    c = pl.program_id(1)
    c_last = (length - 1) // t
    v_last = length - c_last * t
    dk = LANE
    nqk = heads * dk
    cdt = BF16 if sub % 16 == 0 else F32

    @pl.when(c == 0)
    def _():
        qkv_ext[pl.ds(0, EXT0), :] = jnp.zeros((EXT0, qkv_ext.shape[1]), F32)
        xb_ext[pl.ds(0, EXT0), :] = jnp.zeros((EXT0, xb_ext.shape[1]), F32)
        qkv_ext[pl.ds(EXT0 - HIST, HIST), :] = dconv0_ref[0, 0]
        xb_ext[pl.ds(EXT0 - HIST, HIST), :] = lconv0_ref[0, 0]
        s_scr[...] = s0_ref[0, 0]
        h_scr[...] = h0_ref[0]

    rowmask = _iota2((t, 1), 0) < (length - c * t)

    x = jnp.where(rowmask, qkv_ref[...], 0.0)
    qkv = _silu(_causal_conv(qkv_ext, x, dnw_ref[...], t))

    @pl.when(c == c_last)
    def _():
        _put_state(dconv_out, ow, qkv_ext[pl.ds(EXT0 + v_last - HIST, HIST), :])

    qkv_ext[pl.ds(EXT0 - HIST, HIST), :] = qkv_ext[pl.ds(EXT0 + t - HIST, HIST), :]

    bg = jnp.where(rowmask, bg_ref[...], 0.0)
    a_log = dnv_ref[0:1, :]
    dt_bias = dnv_ref[1:2, :]
    norm_g = dnv_ref[2:3, :]
    beta = jnp.where(rowmask, jax.nn.sigmoid(bg[:, :LANE]), 0.0)
    g = jnp.where(rowmask, -jnp.exp(a_log) * jax.nn.softplus(bg[:, LANE:] + dt_bias), 0.0)
    z = z_ref[...]
    r = heads * sub
    row = _iota2((r, r), 0)
    col = _iota2((r, r), 1)
    same = (row // sub) == (col // sub)
    rows = _iota2((sub, 1), 0)
    nsteps = max(1, (sub - 1).bit_length())
    l2n = lambda a: a * lax.rsqrt(jnp.sum(a * a, axis=-1, keepdims=True) + RMS_EPS)
    for j in range(t // sub):
        sl = slice(j * sub, (j + 1) * sub)
        qkv_j, beta_j, mask_j, z_j = qkv[sl], beta[sl], rowmask[sl], z[sl]
        gc = g[sl]
        sh = 1
        while sh < sub:
            gc = gc + jnp.where(rows >= sh, pltpu.roll(gc, sh, 0), 0.0)
            sh *= 2
        stack = lambda f: jnp.concatenate([f(h) for h in range(heads)], axis=0)
        q = stack(lambda h: l2n(qkv_j[:, h * dk:(h + 1) * dk]) * (dk ** -0.5))
        k = stack(lambda h: l2n(qkv_j[:, nqk + h * dk:nqk + (h + 1) * dk]))
        v = stack(lambda h: qkv_j[:, 2 * nqk + h * dk:2 * nqk + (h + 1) * dk])
        beta_c = stack(lambda h: beta_j[:, h:h + 1])
        gc_c = stack(lambda h: gc[:, h:h + 1])
        gc_r = jnp.sum(jnp.where(row == col, gc_c, 0.0), axis=0, keepdims=True)
        decay = jnp.where(same & (row >= col), jnp.exp(gc_c - gc_r), 0.0)
        eg = jnp.exp(gc_c)
        kb = k * beta_c
        kc = k.astype(cdt)
        a_mat = jnp.where(same & (row > col), _dot_nt(kb.astype(cdt), kc) * decay, 0.0)
        y = jnp.concatenate([v * beta_c, kb * eg], axis=-1)
        p = -a_mat
        for i in range(nsteps):
            pc = p.astype(cdt)
            y = y + _dot(pc, y.astype(cdt))
            if i + 1 < nsteps:
                p = _dot(pc, pc)
        u = y[:, :dk]
        w = y[:, dk:]
        qe = q * eg
        ws_qs = [_dot(jnp.concatenate([w[h * sub:(h + 1) * sub], qe[h * sub:(h + 1) * sub]], axis=0).astype(cdt),
                      s_scr[h].astype(cdt)) for h in range(heads)]
        v_new = u - jnp.concatenate([a[:sub] for a in ws_qs], axis=0)
        attn = _dot_nt(q.astype(cdt), kc) * decay
        o = jnp.concatenate([a[sub:] for a in ws_qs], axis=0) + _dot(attn.astype(cdt), v_new.astype(cdt))
        o = o * lax.rsqrt(jnp.mean(o * o, axis=-1, keepdims=True) + RMS_EPS) * norm_g
        for h in range(heads):
            g_last = gc[sub - 1:sub, h:h + 1]
            k_dec = (k[h * sub:(h + 1) * sub] * jnp.exp(g_last - gc[:, h:h + 1])).astype(cdt)
            s_scr[h] = s_scr[h] * jnp.exp(g_last) + _dot_tn(k_dec, v_new[h * sub:(h + 1) * sub].astype(cdt))
            mixed_ref[sl, h * dk:(h + 1) * dk] = jnp.where(
                mask_j, o[h * sub:(h + 1) * sub] * _silu(z_j[:, h * dk:(h + 1) * dk]), 0.0)

    xb = jnp.where(rowmask, xb_ref[...], 0.0)
    xc = _causal_conv(xb_ext, xb, lw_ref[...], t) + lv_ref[0:1, :]

    @pl.when(c == c_last)
    def _():
        _put_state(lconv_out, ow, xb_ext[pl.ds(EXT0 + v_last - HIST, HIST), :])

    xb_ext[pl.ds(EXT0 - HIST, HIST), :] = xb_ext[pl.ds(EXT0 + t - HIST, HIST), :]
    xcb = xc.astype(BF16)
    r = jax.nn.sigmoid(_dot(xcb, wa_ref[...]) + lv_ref[1:2, :])
    gi = jax.nn.sigmoid(_dot(xcb, wx_ref[...]) + lv_ref[2:3, :])
    log_a = jnp.where(rowmask, -LRU_C * r * jax.nn.softplus(-lv_ref[3:4, :]), 0.0)
    a = jnp.exp(log_a)
    uu = jnp.where(rowmask, jnp.sqrt(1.0 - a * a) * (gi * xc), 0.0)
    rows = _iota2((t, 1), 0)
    sh = 1
    while sh < t:
        keep = rows >= sh
        uu = jnp.where(keep, a * pltpu.roll(uu, sh, 0) + uu, uu)
        a = jnp.where(keep, a * pltpu.roll(a, sh, 0), a)
        sh *= 2
    hh = uu + a * h_scr[...]
    h_scr[...] = hh[t - 1:t, :]
    yb = yb_ref[...]
    gelu = 0.5 * yb * (1.0 + jnp.tanh(math.sqrt(2.0 / math.pi) * (yb + 0.044715 * (yb * yb * yb))))
    mixed_ref[:, nqk:] = jnp.where(rowmask, gelu * hh, 0.0)

    @pl.when(c == c_last)
    def _():
        _put_state(s_out, ow, s_scr[...])
        _put_state(h_out, ow, h_scr[...])


def _state_out_specs(shapes, prev, lo, n_l, batch):
    lead = n_l if prev is None else 1
    first = 0 if prev is None else lo
    specs = tuple(pl.BlockSpec((lead, 1) + sh, functools.partial(lambda b, c, nz: (first, b) + (0,) * nz, nz=len(sh)))
                  for sh in shapes)
    out_shape = tuple(jax.ShapeDtypeStruct((n_l, batch) + sh, F32) for sh in shapes)
    return specs, out_shape, (lo if prev is None else 0)


def _ab_mixer(proj, dconv0, s0, lconv0, h0, wts, prev, *, t, sub, length, batch, cpb, rb0, li, lo, n_l):
    dnw, dnv, lw, lv, wa, wx = wts
    heads = s0.shape[2]
    nqk = heads * LANE
    nv = heads * s0.shape[4]
    lw_ch = lw.shape[1]
    dmix = nv + lw_ch
    rmap = lambda j: (lambda b, c: (rb0 + b * cpb + c, j))
    full2 = lambda b, c: (0, 0)
    in_specs = [
        pl.BlockSpec((t, 3 * nqk), rmap(0)),
        pl.BlockSpec((t, nv), rmap(3 * nqk // nv)),
        pl.BlockSpec((t, lw_ch), rmap((3 * nqk + nv) // lw_ch)),
        pl.BlockSpec((t, lw_ch), rmap((3 * nqk + nv) // lw_ch + 1)),
        pl.BlockSpec((t, 2 * LANE), rmap((3 * nqk + nv + 2 * lw_ch) // (2 * LANE))),
        pl.BlockSpec((1, 1, HIST, 3 * nqk), lambda b, c: (li, b, 0, 0)),
        pl.BlockSpec((1, 1) + s0.shape[2:], lambda b, c: (li, b, 0, 0, 0)),
        pl.BlockSpec((1, 1, HIST, lw_ch), lambda b, c: (li, b, 0, 0)),
        pl.BlockSpec((1, 1, lw_ch), lambda b, c: (b, 0, 0)),
        pl.BlockSpec(dnw.shape, full2), pl.BlockSpec(dnv.shape, full2),
        pl.BlockSpec(lw.shape, full2), pl.BlockSpec(lv.shape, full2),
        pl.BlockSpec(wa.shape, full2), pl.BlockSpec(wx.shape, full2),
    ]
    args = [proj, proj, proj, proj, proj, dconv0, s0, lconv0, h0, dnw, dnv, lw, lv, wa, wx]
    st_specs, st_shapes, ow = _state_out_specs(
        (s0.shape[2:], (HIST, 3 * nqk), (1, lw_ch), (HIST, lw_ch)), prev, lo, n_l, batch)
    aliases = {}
    if prev is not None:
        in_specs += [pl.BlockSpec(memory_space=pl.ANY)] * len(prev)
        aliases = {len(args) + j: 1 + j for j in range(len(prev))}
        args += list(prev)
    return pl.pallas_call(
        functools.partial(_ab_kernel, t=t, sub=sub, length=length, heads=heads, ow=ow),
        grid=(batch, cpb),
        in_specs=in_specs,
        out_specs=(pl.BlockSpec((t, dmix), lambda b, c: (b * cpb + c, 0)),) + st_specs,
        out_shape=(jax.ShapeDtypeStruct((batch * cpb * t, dmix), F32),) + st_shapes,
        scratch_shapes=[pltpu.VMEM((t + EXT0, 3 * nqk), F32), pltpu.VMEM((t + EXT0, lw_ch), F32),
                        pltpu.VMEM(s0.shape[2:], F32), pltpu.VMEM((1, lw_ch), F32)],
        input_output_aliases=aliases,
        compiler_params=pltpu.CompilerParams(dimension_semantics=("parallel", "arbitrary"),
                                             vmem_limit_bytes=VMEM_LIMIT),
        name="ab_mixer")(*args)


def _ssd_kernel(z_ref, xs_ref, bc_ref, dt_ref, xconv0_ref, bconv0_ref, s0_ref,
                wxs_ref, wbc_ref, bxs_ref, bbc_ref, vec_ref, ng_ref, *rest,
                t, length, heads, groups, ow):
    mixed_ref, s_out, conv_out, xs_ext, bc_ext, s_scr, ybuf = rest[-7:]
    c = pl.program_id(1)
    c_last = (length - 1) // t
    v_last = length - c_last * t
    inner = xs_ref.shape[1]
    p = inner // heads
    n = s_scr.shape[2]
    hpg = heads // groups
    gw = inner // groups
    cdt = BF16 if t % 16 == 0 else F32

    @pl.when(c == 0)
    def _():
        xs_ext[pl.ds(0, EXT0), :] = jnp.zeros((EXT0, inner), F32)
        bc_ext[pl.ds(0, EXT0), :] = jnp.zeros((EXT0, inner), F32)
        xs_ext[pl.ds(EXT0 - HIST, HIST), :] = xconv0_ref[0, 0]
        bc_ext[pl.ds(EXT0 - HIST, HIST), :] = bconv0_ref[0, 0]
        s_scr[...] = s0_ref[0, 0]

    rowmask = _iota2((t, 1), 0) < (length - c * t)
    xs = _silu(_causal_conv(xs_ext, jnp.where(rowmask, xs_ref[...], 0.0), wxs_ref[...], t) + bxs_ref[...])
    bc = _silu(_causal_conv(bc_ext, jnp.where(rowmask, bc_ref[...], 0.0), wbc_ref[...], t) + bbc_ref[...])

    @pl.when(c == c_last)
    def _():
        _put_state(conv_out, ow, xs_ext[pl.ds(EXT0 + v_last - HIST, HIST), :], (slice(None), slice(0, inner)))
        _put_state(conv_out, ow, bc_ext[pl.ds(EXT0 + v_last - HIST, HIST), :], (slice(None), slice(inner, 2 * inner)))

    xs_ext[pl.ds(EXT0 - HIST, HIST), :] = xs_ext[pl.ds(EXT0 + t - HIST, HIST), :]
    bc_ext[pl.ds(EXT0 - HIST, HIST), :] = bc_ext[pl.ds(EXT0 + t - HIST, HIST), :]

    dt = jnp.where(rowmask, jax.nn.softplus(jnp.where(rowmask, dt_ref[...], 0.0) + vec_ref[0:1, :]), 0.0)
    da = dt * (-jnp.exp(vec_ref[1:2, :]))
    d_skip = vec_ref[2:3, :]
    cs, cst = _cumsum_rows(da)
    ecs = jnp.exp(cs)
    e_last = jnp.exp(cs[t - 1:t, :])
    w_dec = jnp.exp(cs[t - 1:t, :] - cs)
    causal = _iota2((t, t), 0) >= _iota2((t, t), 1)
    gn = groups * n
    lane_g = _iota2((t, gw), 1)
    row_g = _iota2((gw, 1), 0)
    for g in range(groups):
        e0 = g * hpg
        bm = bc[:, g * n:(g + 1) * n].astype(cdt)
        cm = bc[:, gn + g * n:gn + (g + 1) * n].astype(cdt)
        cb = _dot_nt(cm, bm)
        xg = xs[:, g * gw:(g + 1) * gw]
        xdt = xg * _expand_heads(dt, e0, hpg, p)
        m_cat = jnp.concatenate(
            [(cb * jnp.where(causal, jnp.exp(cs[:, e0 + j:e0 + j + 1] - cst[e0 + j:e0 + j + 1, :]), 0.0)).astype(cdt)
             for j in range(hpg)], axis=1)
        x_bd = jnp.concatenate(
            [jnp.where((lane_g >= j * p) & (lane_g < (j + 1) * p), xdt, 0.0) for j in range(hpg)], axis=0).astype(cdt)
        sg = s_scr[pl.ds(e0, hpg)].reshape(gw, n)
        y = _dot(m_cat, x_bd) + _dot_nt(cm, sg.astype(cdt)) * _expand_heads(ecs, e0, hpg, p)
        el = e_last[:, e0:e0 + 1]
        for j in range(1, hpg):
            el = jnp.where(row_g >= j * p, e_last[:, e0 + j:e0 + j + 1], el)
        s_new = sg * el + _dot_tn((xdt * _expand_heads(w_dec, e0, hpg, p)).astype(cdt), bm)
        s_scr[pl.ds(e0, hpg)] = s_new.reshape(hpg, p, n)
        ybuf[:, g * gw:(g + 1) * gw] = y + xg * _expand_heads(d_skip, e0, hpg, p)
    yz = ybuf[...] * _silu(z_ref[...])
    ng = ng_ref[...]
    for g in range(groups):
        seg = yz[:, g * gw:(g + 1) * gw]
        seg = seg * lax.rsqrt(jnp.mean(seg * seg, axis=-1, keepdims=True) + RMS_EPS) * ng[:, g * gw:(g + 1) * gw]
        mixed_ref[:, g * gw:(g + 1) * gw] = jnp.where(rowmask, seg, 0.0)

    @pl.when(c == c_last)
    def _():
        _put_state(s_out, ow, s_scr[...])


def _ssd_mixer(proj, conv0, s0, wts, prev, *, t, length, batch, cpb, rb0, groups, li, lo, n_l):
    cw, cb, vec, ng = wts
    heads = s0.shape[2]
    inner = heads * s0.shape[3]
    rmap = lambda j: (lambda b, c: (rb0 + b * cpb + c, j))
    in_specs = [
        pl.BlockSpec((t, inner), rmap(0)), pl.BlockSpec((t, inner), rmap(1)), pl.BlockSpec((t, inner), rmap(2)),
        pl.BlockSpec((t, LANE), rmap(3 * inner // LANE)),
        pl.BlockSpec((1, 1, HIST, inner), lambda b, c: (li, b, 0, 0)),
        pl.BlockSpec((1, 1, HIST, inner), lambda b, c: (li, b, 0, 1)),
        pl.BlockSpec((1, 1) + s0.shape[2:], lambda b, c: (li, b, 0, 0, 0)),
        pl.BlockSpec((CONV_W, inner), lambda b, c: (0, 0)), pl.BlockSpec((CONV_W, inner), lambda b, c: (0, 1)),
        pl.BlockSpec((1, inner), lambda b, c: (0, 0)), pl.BlockSpec((1, inner), lambda b, c: (0, 1)),
        pl.BlockSpec(vec.shape, lambda b, c: (0, 0)), pl.BlockSpec(ng.shape, lambda b, c: (0, 0)),
    ]
    args = [proj, proj, proj, proj, conv0, conv0, s0, cw, cw, cb, cb, vec, ng]
    st_specs, st_shapes, ow = _state_out_specs((s0.shape[2:], (HIST, conv0.shape[3])), prev, lo, n_l, batch)
    aliases = {}
    if prev is not None:
        in_specs += [pl.BlockSpec(memory_space=pl.ANY)] * len(prev)
        aliases = {len(args) + j: 1 + j for j in range(len(prev))}
        args += list(prev)
    return pl.pallas_call(
        functools.partial(_ssd_kernel, t=t, length=length, heads=heads, groups=groups, ow=ow),
        grid=(batch, cpb),
        in_specs=in_specs,
        out_specs=(pl.BlockSpec((t, inner), lambda b, c: (b * cpb + c, 0)),) + st_specs,
        out_shape=(jax.ShapeDtypeStruct((batch * cpb * t, inner), F32),) + st_shapes,
        scratch_shapes=[pltpu.VMEM((t + EXT0, inner), F32), pltpu.VMEM((t + EXT0, inner), F32),
                        pltpu.VMEM(s0.shape[2:], F32), pltpu.VMEM((t, inner), F32)],
        input_output_aliases=aliases,
        compiler_params=pltpu.CompilerParams(dimension_semantics=("parallel", "arbitrary"),
                                             vmem_limit_bytes=VMEM_LIMIT),
        name="ssd_mixer")(*args)


def _layer_norm(h, g, b):
    mu = jnp.mean(h, axis=-1, keepdims=True)
    d = h - mu
    var = jnp.mean(d * d, axis=-1, keepdims=True)
    return d * lax.rsqrt(var + LN_EPS) * g + b


def _col_to_rows(v):
    eye = _iota2((LANE, LANE), 0) == _iota2((LANE, LANE), 1)
    rows = [jnp.sum(jnp.where(eye, v[i * LANE:(i + 1) * LANE, :], 0.0), axis=0, keepdims=True)
            for i in range(v.shape[0] // LANE)]
    return jnp.concatenate(rows, axis=0)


def _outln_kernel(yp_ref, ys_ref, x_ref, w_ref, ln_ref, wr_ref, br_ref, o_ref, bkt_ref, rank_ref, cnt_ref,
                  ybuf, cnt_scr, *, alpha, n_groups, epg, nbp, n_buckets):
    d = x_ref.shape[1]
    i = pl.program_id(0)

    @pl.when(i == 0)
    def _():
        cnt_scr[...] = jnp.zeros_like(cnt_scr)

    @pl.when(i < nbp)
    def _():
        ybuf[...] = yp_ref[...].astype(BF16)

    @pl.when(i >= nbp)
    def _():
        ybuf[...] = ys_ref[...].astype(BF16)

    acc = _dot(ybuf[...], w_ref[...])
    xn = _layer_norm(alpha * x_ref[...] + acc, ln_ref[0:1, :], ln_ref[1:2, :])
    o_ref[:, :d] = xn
    xh = xn.astype(BF16)
    xl = (xn - xh.astype(F32)).astype(BF16)
    hw = _dot(xh, wr_ref[...])
    logits = hw[:, :ROUTE_LANES] + hw[:, ROUTE_LANES:] + _dot(xl, wr_ref[:, :ROUTE_LANES]) + br_ref[...]
    tm = logits.shape[0]
    lane = _iota2((tm, ROUTE_LANES), 1)
    ninf = -jnp.inf
    gl = jnp.where(lane < n_groups, logits, ninf)
    gmax = jnp.max(gl, axis=-1, keepdims=True)
    gidx = jnp.min(jnp.where(gl == gmax, lane, ROUTE_LANES), axis=-1, keepdims=True)
    g_w = 1.0 / jnp.sum(jnp.where(lane < n_groups, jnp.exp(logits - gmax), 0.0), axis=-1, keepdims=True)
    lo = n_groups + epg * gidx
    el = jnp.where((lane >= lo) & (lane < lo + epg), logits, ninf)
    m1 = jnp.max(el, axis=-1, keepdims=True)
    i1 = jnp.min(jnp.where(el == m1, lane, ROUTE_LANES), axis=-1, keepdims=True)
    el2 = jnp.where(lane == i1, ninf, el)
    m2 = jnp.max(el2, axis=-1, keepdims=True)
    i2 = jnp.min(jnp.where(el2 == m2, lane, ROUTE_LANES), axis=-1, keepdims=True)
    r = jnp.exp(m2 - m1)
    w1 = g_w / (1.0 + r)
    w2 = g_w * r / (1.0 + r)
    e1 = i1 - lo
    e2 = i2 - lo
    first = e1 < e2
    ea = jnp.where(first, e1, e2)
    eb = jnp.where(first, e2, e1)
    bucket = gidx * PAIR_SLOTS + ea * epg + eb
    ga = jnp.where(first, w1, w2)
    gb = jnp.where(first, w2, w1)
    o_ref[:, d:] = jnp.where(lane == 1, ga, jnp.where(lane == 2, gb, 0.0))
    onehot = _iota2((tm, n_buckets), 1) == bucket
    oh = jnp.where(onehot, 1.0, 0.0)
    earlier = jnp.where(_iota2((tm, tm), 0) > _iota2((tm, tm), 1), 1.0, 0.0).astype(BF16)
    before = _dot(earlier, oh.astype(BF16)) + cnt_scr[...]
    rank = jnp.sum(jnp.where(onehot, before, 0.0), axis=-1, keepdims=True)
    cnt_scr[...] = cnt_scr[...] + jnp.sum(oh, axis=0, keepdims=True)
    bkt_ref[0] = _col_to_rows(bucket.astype(F32)).astype(jnp.int32)
    rank_ref[0] = _col_to_rows(rank).astype(jnp.int32)
    cnt_ref[...] = jnp.broadcast_to(cnt_scr[...], cnt_ref.shape).astype(jnp.int32)


def _outln(mixed_p, mixed_s, x, w, ln, wr, br, *, n, tm, alpha, n_groups, epg, n_buckets):
    d = x.shape[1]
    dm = mixed_p.shape[1]
    nbp = mixed_p.shape[0] // tm
    nb = n // tm
    return pl.pallas_call(
        functools.partial(_outln_kernel, alpha=alpha, n_groups=n_groups, epg=epg, nbp=nbp, n_buckets=n_buckets),
        grid=(nb,),
        in_specs=[pl.BlockSpec((tm, dm), lambda i: (jnp.minimum(i, nbp - 1), 0)),
                  pl.BlockSpec((tm, dm), lambda i: (jnp.maximum(i - nbp, 0), 0)),
                  pl.BlockSpec((tm, d), lambda i: (i, 0)),
                  pl.BlockSpec((dm, d), lambda i: (0, 0), pipeline_mode=pl.Buffered(1)),
                  pl.BlockSpec(ln.shape, lambda i: (0, 0)),
                  pl.BlockSpec(wr.shape, lambda i: (0, 0)),
                  pl.BlockSpec(br.shape, lambda i: (0, 0))],
        out_specs=(pl.BlockSpec((tm, d + ROUTE_LANES), lambda i: (i, 0)),
                   pl.BlockSpec((1, tm // LANE, LANE), lambda i: (i, 0, 0)),
                   pl.BlockSpec((1, tm // LANE, LANE), lambda i: (i, 0, 0)),
                   pl.BlockSpec((8, n_buckets), lambda i: (0, 0))),
        out_shape=(jax.ShapeDtypeStruct((n, d + ROUTE_LANES), F32),
                   jax.ShapeDtypeStruct((nb, tm // LANE, LANE), jnp.int32),
                   jax.ShapeDtypeStruct((nb, tm // LANE, LANE), jnp.int32),
                   jax.ShapeDtypeStruct((8, n_buckets), jnp.int32)),
        scratch_shapes=[pltpu.VMEM((tm, dm), BF16), pltpu.VMEM((1, n_buckets), F32)],
        compiler_params=pltpu.CompilerParams(dimension_semantics=("arbitrary",), vmem_limit_bytes=VMEM_LIMIT),
        name="outproj_ln_route")(mixed_p, mixed_s, x, w, ln, wr, br)


def _moe_kernel(c8_ref, ea_ref, eb_ref, src_ref, nxt_ref, x_hbm, w1a_ref, w3a_ref, w2a_ref, w1b_ref, w3b_ref,
                w2b_ref, ln_ref, o_hbm, xbuf, obuf, gsem, ssem, *, alpha, tr, n_rows):
    t = pl.program_id(0)
    nt = pl.num_programs(0)
    d = obuf.shape[2]
    slot = lax.rem(t, 2)
    other = 1 - slot
    c8_t = c8_ref[t]
    c8_next = jnp.where(t + 1 < nt, c8_ref[jnp.minimum(t + 1, nt - 1)], 0)

    def gather(idx_ref, s, c8):
        for g in range(tr // DMA_ROWS):
            @pl.when(g < c8)
            def _():
                for u in range(DMA_ROWS):
                    r = g * DMA_ROWS + u
                    pltpu.make_async_copy(x_hbm.at[pl.ds(jnp.minimum(idx_ref[0, 0, r], n_rows - 1), 1), :],
                                          xbuf.at[s, pl.ds(r, 1), :], gsem.at[s]).start(priority=u % 2)

    def scatter(s, c8):
        for g in range(tr // DMA_ROWS):
            @pl.when(g < c8)
            def _():
                for u in range(DMA_ROWS):
                    r = g * DMA_ROWS + u
                    pltpu.make_async_copy(obuf.at[s, pl.ds(r, 1), :],
                                          o_hbm.at[pl.ds(src_ref[0, 0, r], 1), :], ssem.at[s]).start(priority=u % 2)

    def wait_groups(desc, c8):
        def body(i, carry):
            desc.wait()
            return carry
        lax.fori_loop(0, c8, body, 0)

    def wait_gather(s, c8):
        wait_groups(pltpu.make_async_copy(x_hbm.at[pl.ds(0, DMA_ROWS), :], xbuf.at[s, pl.ds(0, DMA_ROWS), :],
                                          gsem.at[s]), c8)

    def wait_scatter(s, c8):
        wait_groups(pltpu.make_async_copy(obuf.at[s, pl.ds(0, DMA_ROWS), :], o_hbm.at[pl.ds(0, DMA_ROWS), :],
                                          ssem.at[s]), c8)

    @pl.when(t == 0)
    def _():
        xbuf[...] = jnp.zeros_like(xbuf)
        obuf[...] = jnp.zeros_like(obuf)
        for s in range(2):
            fill = pltpu.make_async_copy(obuf.at[s], o_hbm.at[pl.ds(n_rows + s * tr, tr), :], ssem.at[s])
            fill.start()
            fill.wait()
        gather(src_ref, 0, c8_t)

    gather(nxt_ref, other, c8_next)

    @pl.when(c8_t > 0)
    def _():
        wait_gather(slot, c8_t)
        x = xbuf[slot, :, :d]
        ga = xbuf[slot, :, d + 1:d + 2]
        gb = xbuf[slot, :, d + 2:d + 3]
        xb = x.astype(BF16)
        ha = _silu(_dot(xb, w1a_ref[0])) * _dot(xb, w3a_ref[0])
        hb = _silu(_dot(xb, w1b_ref[0])) * _dot(xb, w3b_ref[0])
        ffn = _dot((ha * ga).astype(BF16), w2a_ref[0]) + _dot((hb * gb).astype(BF16), w2b_ref[0])
        y = _layer_norm(alpha * x + ffn, ln_ref[0:1, :], ln_ref[1:2, :])

        @pl.when(t >= 2)
        def _():
            wait_scatter(slot, c8_ref[jnp.maximum(t - 2, 0)])

        obuf[slot] = y
        scatter(slot, c8_t)

        @pl.when(c8_next == 0)
        def _():
            wait_scatter(slot, c8_t)

            @pl.when(t >= 1)
            def _():
                wait_scatter(other, c8_ref[jnp.maximum(t - 1, 0)])


def _moe(x1r, src, c8, ea, eb, w1, w3, w2, ln, *, alpha, tr):
    n, dr = x1r.shape
    d = dr - ROUTE_LANES
    nt = c8.shape[0]
    f = w1.shape[2]
    wa = lambda t, v, a, b: (a[t], 0, 0)
    wb = lambda t, v, a, b: (b[t], 0, 0)
    grid_spec = pltpu.PrefetchScalarGridSpec(
        num_scalar_prefetch=3,
        grid=(nt,),
        in_specs=[pl.BlockSpec((1, 1, tr), lambda t, v, a, b: (t, 0, 0), memory_space=pltpu.SMEM),
                  pl.BlockSpec((1, 1, tr), lambda t, v, a, b: (jnp.minimum(t + 1, nt - 1), 0, 0),
                               memory_space=pltpu.SMEM),
                  pl.BlockSpec(memory_space=pl.ANY),
                  pl.BlockSpec((1, d, f), wa), pl.BlockSpec((1, d, f), wa), pl.BlockSpec((1, f, d), wa),
                  pl.BlockSpec((1, d, f), wb), pl.BlockSpec((1, d, f), wb), pl.BlockSpec((1, f, d), wb),
                  pl.BlockSpec(ln.shape, lambda t, v, a, b: (0, 0))],
        out_specs=pl.BlockSpec(memory_space=pl.ANY),
        scratch_shapes=[pltpu.VMEM((2, tr, dr), F32), pltpu.VMEM((2, tr, d), F32),
                        pltpu.SemaphoreType.DMA((2,)), pltpu.SemaphoreType.DMA((2,))])
    return pl.pallas_call(
        functools.partial(_moe_kernel, alpha=alpha, tr=tr, n_rows=n),
        grid_spec=grid_spec,
        out_shape=jax.ShapeDtypeStruct((n + 2 * tr, d), F32),
        compiler_params=pltpu.CompilerParams(dimension_semantics=("arbitrary",), vmem_limit_bytes=VMEM_LIMIT),
        name="moe_routed")(c8, ea, eb, src, src, x1r, w1, w3, w2, w1, w3, w2, ln)


def _route_schedule(bucket, rank, counts, *, n, tr, nt, epg, e_off):
    nbk = counts.shape[0]
    tiles_b = (counts + tr - 1) // tr
    tile_end = jnp.cumsum(tiles_b)
    pstart = (tile_end - tiles_b) * tr
    hit = bucket[:, None] == jnp.arange(nbk, dtype=jnp.int32)[None, :]
    dest = jnp.sum(jnp.where(hit, pstart[None, :], 0), axis=1) + rank
    slots = jnp.arange(nt * tr, dtype=jnp.int32)
    trash = n + ((slots // tr) % 2) * tr + slots % tr
    src = trash.at[dest].set(jnp.arange(n, dtype=jnp.int32))
    tiles = jnp.arange(nt, dtype=jnp.int32)
    tb = jnp.sum(tile_end[None, :] <= jnp.minimum(tiles, tile_end[-1] - 1)[:, None], axis=1).astype(jnp.int32)
    own = tb[:, None] == jnp.arange(nbk, dtype=jnp.int32)[None, :]
    pick = lambda tab: jnp.sum(jnp.where(own, tab[None, :], 0), axis=1)
    rows_t = jnp.clip(pick(counts) - (tiles - pick(tile_end - tiles_b)) * tr, 0, tr)
    c8 = jnp.where(tiles < tile_end[-1], (rows_t + DMA_ROWS - 1) // DMA_ROWS, 0)
    grp = tb // PAIR_SLOTS
    pair = tb % PAIR_SLOTS
    ea = e_off + grp * epg + pair // epg
    eb = e_off + grp * epg + pair % epg
    return src.reshape(nt, 1, tr), c8.astype(jnp.int32), ea.astype(jnp.int32), eb.astype(jnp.int32)


def _pad_cols(w, width):
    return jnp.pad(w, ((0, 0), (0, width - w.shape[1])))


def _pad_vec(v, width=LANE):
    return jnp.pad(v.astype(F32), (0, width - v.shape[0]))[None, :]


def _rows8(*vecs):
    m = jnp.stack([v.astype(F32) for v in vecs], axis=0)
    return jnp.pad(m, ((0, 8 - m.shape[0]), (0, 0)))


def _block_diag(w):
    nb, bd, _ = w.shape
    eye = jnp.eye(nb, dtype=w.dtype)
    return jnp.einsum("nde,nm->ndme", w, eye).reshape(nb * bd, nb * bd)


def kernel(x_prompt, x_sample, state_delta, state_delta_conv, state_lru, state_lru_conv, state_ssm, state_ssm_conv,
           meta, w_in_ab, dn_conv_w, dn_a_log, dn_dt_bias, dn_norm_g, lru_conv_w, lru_conv_b, lru_w_a, lru_b_a,
           lru_w_x, lru_b_x, lru_lam, w_out_ab, w_in_ssd, ssd_conv_w, ssd_conv_b, ssd_dt_bias, ssd_a_log, ssd_d,
           ssd_norm_g, w_out_ssd, ln_g, ln_b, moe_w_group, moe_b_group, moe_w_expert, moe_b_expert, moe_w1, moe_w3,
           moe_w2):
    bp, seq, d = x_prompt.shape
    bs, ls, _ = x_sample.shape
    depth = ln_g.shape[0]
    alpha = (2.0 * depth) ** 0.25
    n_groups, epg = moe_w1.shape[1], moe_w1.shape[2]
    heads_dn = dn_a_log.shape[1]
    nqk = heads_dn * state_delta.shape[3]
    nv = heads_dn * state_delta.shape[4]
    lru_w = lru_lam.shape[1]
    heads_ssd = ssd_a_log.shape[1]
    inner = heads_ssd * state_ssm.shape[3]
    n_state = state_ssm.shape[4]
    ssd_groups = (ssd_conv_w.shape[2] - inner) // (2 * n_state)
    assert nqk == nv and state_delta.shape[3] == LANE and state_delta.shape[4] == LANE and n_state == LANE

    lp_len = N_META + seq
    chunk_lcm = math.lcm(DN_BLOCK, SSD_CHUNK)
    lpad = -(-lp_len // chunk_lcm) * chunk_lcm
    np_rows = bp * lpad
    ns_rows = bs * ls
    ntot = np_rows + ns_rows
    tm = math.gcd(math.gcd(np_rows, ns_rows), ROW_TILE)
    assert tm % LANE == 0 and np_rows % ls == 0 and ls % 8 == 0 and ls >= HIST
    assert (lp_len - 1) % DN_BLOCK + 1 >= HIST and (lp_len - 1) % SSD_CHUNK + 1 >= HIST

    xp = jnp.concatenate([jnp.broadcast_to(meta.astype(F32), (bp, N_META, d)), x_prompt,
                          jnp.zeros((bp, lpad - lp_len, d), F32)], axis=1)
    x = jnp.concatenate([xp.reshape(np_rows, d), x_sample.reshape(ns_rows, d)], axis=0)

    tr = MOE_TILE
    n_buckets = n_groups * PAIR_SLOTS
    n_pairs = n_groups * (epg * (epg - 1) // 2)
    nt = -(-(ntot + n_pairs * (tr - 1)) // tr)
    n_exp = n_groups * epg
    w1 = moe_w1.reshape(depth * n_exp, d, -1).astype(BF16)
    w3 = moe_w3.reshape(depth * n_exp, d, -1).astype(BF16)
    w2 = moe_w2.reshape(depth * n_exp, -1, d).astype(BF16)

    zeros = lambda *s: jnp.zeros(s, F32)
    n_ab, n_ssd = (depth + 1) // 2, depth // 2
    st_ab_p = st_ab_s = st_ssd_p = st_ssd_s = None
    for layer in range(depth):
        i = layer // 2
        if layer % 2 == 0:
            wi = w_in_ab[i]
            o_b = 3 * nqk + nv
            w_in = jnp.concatenate([wi[:, :o_b], wi[:, o_b + 2 * heads_dn:],
                                    _pad_cols(wi[:, o_b:o_b + heads_dn], LANE),
                                    _pad_cols(wi[:, o_b + heads_dn:o_b + 2 * heads_dn], LANE)], axis=1).astype(BF16)
            proj = _inproj(x, w_in, tm, ntot)
            wts = (dn_conv_w[i], _rows8(_pad_vec(dn_a_log[i])[0], _pad_vec(dn_dt_bias[i])[0], dn_norm_g[i]),
                   lru_conv_w[i], _rows8(lru_conv_b[i], lru_b_a[i], lru_b_x[i], lru_lam[i]),
                   _block_diag(lru_w_a[i]).astype(BF16), _block_diag(lru_w_x[i]).astype(BF16))
            mixed_p, *st_ab_p = _ab_mixer(
                proj, zeros(1, bp, HIST, 3 * nqk), zeros(1, bp, *state_delta.shape[2:]), zeros(1, bp, HIST, lru_w),
                zeros(bp, 1, lru_w), wts, st_ab_p, t=DN_BLOCK, sub=DN_CHUNK, length=lp_len, batch=bp,
                cpb=lpad // DN_BLOCK, rb0=0, li=0, lo=i, n_l=n_ab)
            mixed_s, *st_ab_s = _ab_mixer(
                proj, state_delta_conv, state_delta, state_lru_conv, state_lru[i][:, None, :], wts, st_ab_s,
                t=ls, sub=ls, length=ls, batch=bs, cpb=1, rb0=np_rows // ls, li=i, lo=i, n_l=n_ab)
            w_out = w_out_ab[i].astype(BF16)
        else:
            wi = w_in_ssd[i]
            e_raw = wi.shape[1]
            w_in = _pad_cols(wi, -(-e_raw // LANE) * LANE).astype(BF16)
            proj = _inproj(x, w_in, tm, ntot)
            wts = (ssd_conv_w[i], ssd_conv_b[i][None, :],
                   _rows8(_pad_vec(ssd_dt_bias[i])[0], _pad_vec(ssd_a_log[i])[0], _pad_vec(ssd_d[i])[0]),
                   ssd_norm_g[i][None, :])
            mixed_p, *st_ssd_p = _ssd_mixer(
                proj, zeros(1, bp, HIST, ssd_conv_w.shape[2]), zeros(1, bp, *state_ssm.shape[2:]), wts, st_ssd_p,
                t=SSD_CHUNK, length=lp_len, batch=bp, cpb=lpad // SSD_CHUNK, rb0=0, groups=ssd_groups,
                li=0, lo=i, n_l=n_ssd)
            mixed_s, *st_ssd_s = _ssd_mixer(
                proj, state_ssm_conv, state_ssm, wts, st_ssd_s,
                t=ls, length=ls, batch=bs, cpb=1, rb0=np_rows // ls, groups=ssd_groups, li=i, lo=i, n_l=n_ssd)
            w_out = w_out_ssd[i].astype(BF16)
        wr = _pad_cols(jnp.concatenate([moe_w_group[layer], moe_w_expert[layer]], axis=1), ROUTE_LANES)
        wr_hi = wr.astype(BF16)
        wr = jnp.concatenate([wr_hi, (wr - wr_hi.astype(F32)).astype(BF16)], axis=1)
        br = _pad_vec(jnp.concatenate([moe_b_group[layer], moe_b_expert[layer]]), ROUTE_LANES)
        x1r, bucket, rank, counts = _outln(mixed_p, mixed_s, x, w_out, _rows8(ln_g[layer, 0], ln_b[layer, 0]), wr, br,
                                           n=ntot, tm=tm, alpha=alpha, n_groups=n_groups, epg=epg,
                                           n_buckets=n_buckets)
        src, c8, ea, eb = _route_schedule(bucket.reshape(ntot), rank.reshape(ntot), counts[0], n=ntot, tr=tr,
                                             nt=nt, epg=epg, e_off=layer * n_exp)
        x = _moe(x1r, src, c8, ea, eb, w1, w3, w2, _rows8(ln_g[layer, 1], ln_b[layer, 1]), alpha=alpha, tr=tr)

    y_prompt = x[:np_rows].reshape(bp, lpad, d)[:, N_META:lp_len]
    y_sample = x[np_rows:ntot].reshape(bs, ls, d)
    ab = lambda st: (st[0], st[1], st[2][:, :, 0], st[3])
    return (y_prompt, y_sample) + ab(st_ab_p) + tuple(st_ssd_p) + ab(st_ab_s) + tuple(st_ssd_s)
```

```python
import functools
import math

import jax
import jax.numpy as jnp
from jax import lax
from jax.experimental import pallas as pl
from jax.experimental.pallas import tpu as pltpu

F32 = jnp.float32
BF16 = jnp.bfloat16
HI = lax.Precision.HIGHEST

LN_EPS = 1e-5
RMS_EPS = 1e-6
LRU_C = 8.0
N_META = 16
CONV_W = 4
HIST = CONV_W - 1

LANE = 128
EXT0 = 8
DN_CHUNK = 64
DN_BLOCK = 128
SSD_CHUNK = 128
ROW_TILE = 512
MOE_TILE = 256
DMA_ROWS = 8
PAIR_SLOTS = 64
ROUTE_LANES = 128
VMEM_LIMIT = 56 * 1024 * 1024


def _silu(x):
    return x * jax.nn.sigmoid(x)


def _dot(a, b, **kw):
    return jnp.dot(a, b, preferred_element_type=F32, **kw)


def _dot_nt(a, b, **kw):
    return lax.dot_general(a, b, (((1,), (1,)), ((), ())), preferred_element_type=F32, **kw)


def _dot_tn(a, b, **kw):
    return lax.dot_general(a, b, (((0,), (0,)), ((), ())), preferred_element_type=F32, **kw)


def _iota2(shape, dim):
    return lax.broadcasted_iota(jnp.int32, shape, dim)


def _cumsum_rows(x):
    t = x.shape[0]
    rows = _iota2((t, 1), 0)
    cs = x
    sh = 1
    while sh < t:
        cs = cs + jnp.where(rows >= sh, pltpu.roll(cs, sh, 0), 0.0)
        sh *= 2
    if t == LANE:
        cst = cs.T
    else:
        eye = (_iota2((LANE, LANE), 0) == _iota2((LANE, LANE), 1)).astype(F32)
        cst = _dot_nt(eye, cs, precision=HI)
    return cs, cst


def _expand_heads(v, e0, nh, p):
    lane = _iota2((v.shape[0], nh * p), 1)
    out = jnp.broadcast_to(v[:, e0:e0 + 1], (v.shape[0], nh * p))
    for j in range(1, nh):
        out = jnp.where(lane >= j * p, v[:, e0 + j:e0 + j + 1], out)
    return out


def _put_state(ref, ow, val, sl=()):
    ref[(ow, 0) + sl] = val
    for j in range(ref.shape[0]):
        if j != ow:
            ref[(j, 0) + sl] = jnp.zeros_like(val)


def _causal_conv(ext_ref, x, w, t):
    ext_ref[pl.ds(EXT0, t), :] = x
    ext = ext_ref[...]
    y = x * w[HIST:CONV_W, :]
    for k in range(HIST):
        y = y + pltpu.roll(ext, HIST - k, 0)[EXT0:EXT0 + t, :] * w[k:k + 1, :]
    return y


def _inproj_kernel(x_ref, w_ref, o_ref, *, tn):
    xb = x_ref[...].astype(BF16)
    for j in range(o_ref.shape[1] // tn):
        o_ref[:, j * tn:(j + 1) * tn] = _dot(xb, w_ref[:, j * tn:(j + 1) * tn])


def _inproj(x, w, tm, n):
    k = x.shape[1]
    e = w.shape[1]
    tn = max(c for c in range(LANE, 1024 + LANE, LANE) if e % c == 0)
    return pl.pallas_call(
        functools.partial(_inproj_kernel, tn=tn),
        grid=(n // tm,),
        in_specs=[pl.BlockSpec((tm, k), lambda i: (i, 0)),
                  pl.BlockSpec((k, e), lambda i: (0, 0), pipeline_mode=pl.Buffered(1))],
        out_specs=pl.BlockSpec((tm, e), lambda i: (i, 0)),
        out_shape=jax.ShapeDtypeStruct((n, e), F32),
        compiler_params=pltpu.CompilerParams(dimension_semantics=("parallel",), vmem_limit_bytes=VMEM_LIMIT),
        name="inproj")(x, w)


def _ab_kernel(qkv_ref, z_ref, xb_ref, yb_ref, bg_ref, dconv0_ref, s0_ref, lconv0_ref, h0_ref,
               dnw_ref, dnv_ref, lw_ref, lv_ref, wa_ref, wx_ref, *rest,
               t, sub, length, heads, ow):
    mixed_ref, s_out, dconv_out, h_out, lconv_out, qkv_ext, xb_ext, s_scr, h_scr = rest[-9:]
    c = pl.program_id(1)
    c_last = (length - 1) // t
    v_last = length - c_last * t
    dk = LANE
    nqk = heads * dk
    cdt = BF16 if sub % 16 == 0 else F32

    @pl.when(c == 0)
    def _():
        qkv_ext[pl.ds(0, EXT0), :] = jnp.zeros((EXT0, qkv_ext.shape[1]), F32)
        xb_ext[pl.ds(0, EXT0), :] = jnp.zeros((EXT0, xb_ext.shape[1]), F32)
        qkv_ext[pl.ds(EXT0 - HIST, HIST), :] = dconv0_ref[0, 0]
        xb_ext[pl.ds(EXT0 - HIST, HIST), :] = lconv0_ref[0, 0]
        s_scr[...] = s0_ref[0, 0]
        h_scr[...] = h0_ref[0]

    rowmask = _iota2((t, 1), 0) < (length - c * t)

    x = jnp.where(rowmask, qkv_ref[...], 0.0)
    qkv = _silu(_causal_conv(qkv_ext, x, dnw_ref[...], t))

    @pl.when(c == c_last)
    def _():
        _put_state(dconv_out, ow, qkv_ext[pl.ds(EXT0 + v_last - HIST, HIST), :])

    qkv_ext[pl.ds(EXT0 - HIST, HIST), :] = qkv_ext[pl.ds(EXT0 + t - HIST, HIST), :]

    bg = jnp.where(rowmask, bg_ref[...], 0.0)
    a_log = dnv_ref[0:1, :]
    dt_bias = dnv_ref[1:2, :]
    norm_g = dnv_ref[2:3, :]
    beta = jnp.where(rowmask, jax.nn.sigmoid(bg[:, :LANE]), 0.0)
    g = jnp.where(rowmask, -jnp.exp(a_log) * jax.nn.softplus(bg[:, LANE:] + dt_bias), 0.0)
    z = z_ref[...]
    r = heads * sub
    row = _iota2((r, r), 0)
    col = _iota2((r, r), 1)
    same = (row // sub) == (col // sub)
    rows = _iota2((sub, 1), 0)
    nsteps = max(1, (sub - 1).bit_length())
    l2n = lambda a: a * lax.rsqrt(jnp.sum(a * a, axis=-1, keepdims=True) + RMS_EPS)
    for j in range(t // sub):
        sl = slice(j * sub, (j + 1) * sub)
        qkv_j, beta_j, mask_j, z_j = qkv[sl], beta[sl], rowmask[sl], z[sl]
        gc = g[sl]
        sh = 1
        while sh < sub:
            gc = gc + jnp.where(rows >= sh, pltpu.roll(gc, sh, 0), 0.0)
            sh *= 2
        stack = lambda f: jnp.concatenate([f(h) for h in range(heads)], axis=0)
        q = stack(lambda h: l2n(qkv_j[:, h * dk:(h + 1) * dk]) * (dk ** -0.5))
        k = stack(lambda h: l2n(qkv_j[:, nqk + h * dk:nqk + (h + 1) * dk]))
        v = stack(lambda h: qkv_j[:, 2 * nqk + h * dk:2 * nqk + (h + 1) * dk])
        beta_c = stack(lambda h: beta_j[:, h:h + 1])
        gc_c = stack(lambda h: gc[:, h:h + 1])
        gc_r = jnp.sum(jnp.where(row == col, gc_c, 0.0), axis=0, keepdims=True)
        decay = jnp.where(same & (row >= col), jnp.exp(gc_c - gc_r), 0.0)
        eg = jnp.exp(gc_c)
        kb = k * beta_c
        kc = k.astype(cdt)
        a_mat = jnp.where(same & (row > col), _dot_nt(kb.astype(cdt), kc) * decay, 0.0)
        y = jnp.concatenate([v * beta_c, kb * eg], axis=-1)
        p = -a_mat
        for i in range(nsteps):
            pc = p.astype(cdt)
            y = y + _dot(pc, y.astype(cdt))
            if i + 1 < nsteps:
                p = _dot(pc, pc)
        u = y[:, :dk]
        w = y[:, dk:]
        qe = q * eg
        ws_qs = [_dot(jnp.concatenate([w[h * sub:(h + 1) * sub], qe[h * sub:(h + 1) * sub]], axis=0).astype(cdt),
                      s_scr[h].astype(cdt)) for h in range(heads)]
        v_new = u - jnp.concatenate([a[:sub] for a in ws_qs], axis=0)
        attn = _dot_nt(q.astype(cdt), kc) * decay
        o = jnp.concatenate([a[sub:] for a in ws_qs], axis=0) + _dot(attn.astype(cdt), v_new.astype(cdt))
        o = o * lax.rsqrt(jnp.mean(o * o, axis=-1, keepdims=True) + RMS_EPS) * norm_g
        for h in range(heads):
            g_last = gc[sub - 1:sub, h:h + 1]
            k_dec = (k[h * sub:(h + 1) * sub] * jnp.exp(g_last - gc[:, h:h + 1])).astype(cdt)
            s_scr[h] = s_scr[h] * jnp.exp(g_last) + _dot_tn(k_dec, v_new[h * sub:(h + 1) * sub].astype(cdt))
            mixed_ref[sl, h * dk:(h + 1) * dk] = jnp.where(
                mask_j, o[h * sub:(h + 1) * sub] * _silu(z_j[:, h * dk:(h + 1) * dk]), 0.0)

    xb = jnp.where(rowmask, xb_ref[...], 0.0)
    xc = _causal_conv(xb_ext, xb, lw_ref[...], t) + lv_ref[0:1, :]

    @pl.when(c == c_last)
    def _():
        _put_state(lconv_out, ow, xb_ext[pl.ds(EXT0 + v_last - HIST, HIST), :])

    xb_ext[pl.ds(EXT0 - HIST, HIST), :] = xb_ext[pl.ds(EXT0 + t - HIST, HIST), :]
    xcb = xc.astype(BF16)
    r = jax.nn.sigmoid(_dot(xcb, wa_ref[...]) + lv_ref[1:2, :])
    gi = jax.nn.sigmoid(_dot(xcb, wx_ref[...]) + lv_ref[2:3, :])
    log_a = jnp.where(rowmask, -LRU_C * r * jax.nn.softplus(-lv_ref[3:4, :]), 0.0)
    a = jnp.exp(log_a)
    uu = jnp.where(rowmask, jnp.sqrt(1.0 - a * a) * (gi * xc), 0.0)
    rows = _iota2((t, 1), 0)
    sh = 1
    while sh < t:
        keep = rows >= sh
        uu = jnp.where(keep, a * pltpu.roll(uu, sh, 0) + uu, uu)
        a = jnp.where(keep, a * pltpu.roll(a, sh, 0), a)
        sh *= 2
    hh = uu + a * h_scr[...]
    h_scr[...] = hh[t - 1:t, :]
    yb = yb_ref[...]
    gelu = 0.5 * yb * (1.0 + jnp.tanh(math.sqrt(2.0 / math.pi) * (yb + 0.044715 * (yb * yb * yb))))
    mixed_ref[:, nqk:] = jnp.where(rowmask, gelu * hh, 0.0)

    @pl.when(c == c_last)
    def _():
        _put_state(s_out, ow, s_scr[...])
        _put_state(h_out, ow, h_scr[...])


def _state_out_specs(shapes, prev, lo, n_l, batch):
    lead = n_l if prev is None else 1
    first = 0 if prev is None else lo
    specs = tuple(pl.BlockSpec((lead, 1) + sh, functools.partial(lambda b, c, nz: (first, b) + (0,) * nz, nz=len(sh)))
                  for sh in shapes)
    out_shape = tuple(jax.ShapeDtypeStruct((n_l, batch) + sh, F32) for sh in shapes)
    return specs, out_shape, (lo if prev is None else 0)


def _ab_mixer(proj, dconv0, s0, lconv0, h0, wts, prev, *, t, sub, length, batch, cpb, rb0, li, lo, n_l):
    dnw, dnv, lw, lv, wa, wx = wts
    heads = s0.shape[2]
    nqk = heads * LANE
    nv = heads * s0.shape[4]
    lw_ch = lw.shape[1]
    dmix = nv + lw_ch
    rmap = lambda j: (lambda b, c: (rb0 + b * cpb + c, j))
    full2 = lambda b, c: (0, 0)
    in_specs = [
        pl.BlockSpec((t, 3 * nqk), rmap(0)),
        pl.BlockSpec((t, nv), rmap(3 * nqk // nv)),
        pl.BlockSpec((t, lw_ch), rmap((3 * nqk + nv) // lw_ch)),
        pl.BlockSpec((t, lw_ch), rmap((3 * nqk + nv) // lw_ch + 1)),
        pl.BlockSpec((t, 2 * LANE), rmap((3 * nqk + nv + 2 * lw_ch) // (2 * LANE))),
        pl.BlockSpec((1, 1, HIST, 3 * nqk), lambda b, c: (li, b, 0, 0)),
        pl.BlockSpec((1, 1) + s0.shape[2:], lambda b, c: (li, b, 0, 0, 0)),
        pl.BlockSpec((1, 1, HIST, lw_ch), lambda b, c: (li, b, 0, 0)),
        pl.BlockSpec((1, 1, lw_ch), lambda b, c: (b, 0, 0)),
        pl.BlockSpec(dnw.shape, full2), pl.BlockSpec(dnv.shape, full2),
        pl.BlockSpec(lw.shape, full2), pl.BlockSpec(lv.shape, full2),
        pl.BlockSpec(wa.shape, full2), pl.BlockSpec(wx.shape, full2),
    ]
    args = [proj, proj, proj, proj, proj, dconv0, s0, lconv0, h0, dnw, dnv, lw, lv, wa, wx]
    st_specs, st_shapes, ow = _state_out_specs(
        (s0.shape[2:], (HIST, 3 * nqk), (1, lw_ch), (HIST, lw_ch)), prev, lo, n_l, batch)
    aliases = {}
    if prev is not None:
        in_specs += [pl.BlockSpec(memory_space=pl.ANY)] * len(prev)
        aliases = {len(args) + j: 1 + j for j in range(len(prev))}
        args += list(prev)
    return pl.pallas_call(
        functools.partial(_ab_kernel, t=t, sub=sub, length=length, heads=heads, ow=ow),
        grid=(batch, cpb),
        in_specs=in_specs,
        out_specs=(pl.BlockSpec((t, dmix), lambda b, c: (b * cpb + c, 0)),) + st_specs,
        out_shape=(jax.ShapeDtypeStruct((batch * cpb * t, dmix), F32),) + st_shapes,
        scratch_shapes=[pltpu.VMEM((t + EXT0, 3 * nqk), F32), pltpu.VMEM((t + EXT0, lw_ch), F32),
                        pltpu.VMEM(s0.shape[2:], F32), pltpu.VMEM((1, lw_ch), F32)],
        input_output_aliases=aliases,
        compiler_params=pltpu.CompilerParams(dimension_semantics=("parallel", "arbitrary"),
                                             vmem_limit_bytes=VMEM_LIMIT),
        name="ab_mixer")(*args)


def _ssd_kernel(z_ref, xs_ref, bc_ref, dt_ref, xconv0_ref, bconv0_ref, s0_ref,
                wxs_ref, wbc_ref, bxs_ref, bbc_ref, vec_ref, ng_ref, *rest,
                t, length, heads, groups, ow):
    mixed_ref, s_out, conv_out, xs_ext, bc_ext, s_scr, ybuf = rest[-7:]
    c = pl.program_id(1)
    c_last = (length - 1) // t
    v_last = length - c_last * t
    inner = xs_ref.shape[1]
    p = inner // heads
    n = s_scr.shape[2]
    hpg = heads // groups
    gw = inner // groups
    cdt = BF16 if t % 16 == 0 else F32

    @pl.when(c == 0)
    def _():
        xs_ext[pl.ds(0, EXT0), :] = jnp.zeros((EXT0, inner), F32)
        bc_ext[pl.ds(0, EXT0), :] = jnp.zeros((EXT0, inner), F32)
        xs_ext[pl.ds(EXT0 - HIST, HIST), :] = xconv0_ref[0, 0]
        bc_ext[pl.ds(EXT0 - HIST, HIST), :] = bconv0_ref[0, 0]
        s_scr[...] = s0_ref[0, 0]

    rowmask = _iota2((t, 1), 0) < (length - c * t)
    xs = _silu(_causal_conv(xs_ext, jnp.where(rowmask, xs_ref[...], 0.0), wxs_ref[...], t) + bxs_ref[...])
    bc = _silu(_causal_conv(bc_ext, jnp.where(rowmask, bc_ref[...], 0.0), wbc_ref[...], t) + bbc_ref[...])

    @pl.when(c == c_last)
    def _():
        _put_state(conv_out, ow, xs_ext[pl.ds(EXT0 + v_last - HIST, HIST), :], (slice(None), slice(0, inner)))
        _put_state(conv_out, ow, bc_ext[pl.ds(EXT0 + v_last - HIST, HIST), :], (slice(None), slice(inner, 2 * inner)))

    xs_ext[pl.ds(EXT0 - HIST, HIST), :] = xs_ext[pl.ds(EXT0 + t - HIST, HIST), :]
    bc_ext[pl.ds(EXT0 - HIST, HIST), :] = bc_ext[pl.ds(EXT0 + t - HIST, HIST), :]

    dt = jnp.where(rowmask, jax.nn.softplus(jnp.where(rowmask, dt_ref[...], 0.0) + vec_ref[0:1, :]), 0.0)
    da = dt * (-jnp.exp(vec_ref[1:2, :]))
    d_skip = vec_ref[2:3, :]
    cs, cst = _cumsum_rows(da)
    ecs = jnp.exp(cs)
    e_last = jnp.exp(cs[t - 1:t, :])
    w_dec = jnp.exp(cs[t - 1:t, :] - cs)
    causal = _iota2((t, t), 0) >= _iota2((t, t), 1)
    gn = groups * n
    lane_g = _iota2((t, gw), 1)
    row_g = _iota2((gw, 1), 0)
    for g in range(groups):
        e0 = g * hpg
        bm = bc[:, g * n:(g + 1) * n].astype(cdt)
        cm = bc[:, gn + g * n:gn + (g + 1) * n].astype(cdt)
        cb = _dot_nt(cm, bm)
        xg = xs[:, g * gw:(g + 1) * gw]
        xdt = xg * _expand_heads(dt, e0, hpg, p)
        m_cat = jnp.concatenate(
            [(cb * jnp.where(causal, jnp.exp(cs[:, e0 + j:e0 + j + 1] - cst[e0 + j:e0 + j + 1, :]), 0.0)).astype(cdt)
             for j in range(hpg)], axis=1)
        x_bd = jnp.concatenate(
            [jnp.where((lane_g >= j * p) & (lane_g < (j + 1) * p), xdt, 0.0) for j in range(hpg)], axis=0).astype(cdt)
        sg = s_scr[pl.ds(e0, hpg)].reshape(gw, n)
        y = _dot(m_cat, x_bd) + _dot_nt(cm, sg.astype(cdt)) * _expand_heads(ecs, e0, hpg, p)
        el = e_last[:, e0:e0 + 1]
        for j in range(1, hpg):
            el = jnp.where(row_g >= j * p, e_last[:, e0 + j:e0 + j + 1], el)
        s_new = sg * el + _dot_tn((xdt * _expand_heads(w_dec, e0, hpg, p)).astype(cdt), bm)
        s_scr[pl.ds(e0, hpg)] = s_new.reshape(hpg, p, n)
        ybuf[:, g * gw:(g + 1) * gw] = y + xg * _expand_heads(d_skip, e0, hpg, p)
    yz = ybuf[...] * _silu(z_ref[...])
    ng = ng_ref[...]
    for g in range(groups):
        seg = yz[:, g * gw:(g + 1) * gw]
        seg = seg * lax.rsqrt(jnp.mean(seg * seg, axis=-1, keepdims=True) + RMS_EPS) * ng[:, g * gw:(g + 1) * gw]
        mixed_ref[:, g * gw:(g + 1) * gw] = jnp.where(rowmask, seg, 0.0)

    @pl.when(c == c_last)
    def _():
        _put_state(s_out, ow, s_scr[...])


def _ssd_mixer(proj, conv0, s0, wts, prev, *, t, length, batch, cpb, rb0, groups, li, lo, n_l):
    cw, cb, vec, ng = wts
    heads = s0.shape[2]
    inner = heads * s0.shape[3]
    rmap = lambda j: (lambda b, c: (rb0 + b * cpb + c, j))
    in_specs = [
        pl.BlockSpec((t, inner), rmap(0)), pl.BlockSpec((t, inner), rmap(1)), pl.BlockSpec((t, inner), rmap(2)),
        pl.BlockSpec((t, LANE), rmap(3 * inner // LANE)),
        pl.BlockSpec((1, 1, HIST, inner), lambda b, c: (li, b, 0, 0)),
        pl.BlockSpec((1, 1, HIST, inner), lambda b, c: (li, b, 0, 1)),
        pl.BlockSpec((1, 1) + s0.shape[2:], lambda b, c: (li, b, 0, 0, 0)),
        pl.BlockSpec((CONV_W, inner), lambda b, c: (0, 0)), pl.BlockSpec((CONV_W, inner), lambda b, c: (0, 1)),
        pl.BlockSpec((1, inner), lambda b, c: (0, 0)), pl.BlockSpec((1, inner), lambda b, c: (0, 1)),
        pl.BlockSpec(vec.shape, lambda b, c: (0, 0)), pl.BlockSpec(ng.shape, lambda b, c: (0, 0)),
    ]
    args = [proj, proj, proj, proj, conv0, conv0, s0, cw, cw, cb, cb, vec, ng]
    st_specs, st_shapes, ow = _state_out_specs((s0.shape[2:], (HIST, conv0.shape[3])), prev, lo, n_l, batch)
    aliases = {}
    if prev is not None:
        in_specs += [pl.BlockSpec(memory_space=pl.ANY)] * len(prev)
        aliases = {len(args) + j: 1 + j for j in range(len(prev))}
        args += list(prev)
    return pl.pallas_call(
        functools.partial(_ssd_kernel, t=t, length=length, heads=heads, groups=groups, ow=ow),
        grid=(batch, cpb),
        in_specs=in_specs,
        out_specs=(pl.BlockSpec((t, inner), lambda b, c: (b * cpb + c, 0)),) + st_specs,
        out_shape=(jax.ShapeDtypeStruct((batch * cpb * t, inner), F32),) + st_shapes,
        scratch_shapes=[pltpu.VMEM((t + EXT0, inner), F32), pltpu.VMEM((t + EXT0, inner), F32),
                        pltpu.VMEM(s0.shape[2:], F32), pltpu.VMEM((t, inner), F32)],
        input_output_aliases=aliases,
        compiler_params=pltpu.CompilerParams(dimension_semantics=("parallel", "arbitrary"),
                                             vmem_limit_bytes=VMEM_LIMIT),
        name="ssd_mixer")(*args)


def _layer_norm(h, g, b):
    mu = jnp.mean(h, axis=-1, keepdims=True)
    d = h - mu
    var = jnp.mean(d * d, axis=-1, keepdims=True)
    return d * lax.rsqrt(var + LN_EPS) * g + b


def _col_to_rows(v):
    eye = _iota2((LANE, LANE), 0) == _iota2((LANE, LANE), 1)
    rows = [jnp.sum(jnp.where(eye, v[i * LANE:(i + 1) * LANE, :], 0.0), axis=0, keepdims=True)
            for i in range(v.shape[0] // LANE)]
    return jnp.concatenate(rows, axis=0)


def _outln_kernel(yp_ref, ys_ref, x_ref, w_ref, ln_ref, wr_ref, br_ref, o_ref, bkt_ref, rank_ref, cnt_ref,
                  ybuf, cnt_scr, *, alpha, n_groups, epg, nbp, n_buckets):
    d = x_ref.shape[1]
    i = pl.program_id(0)

    @pl.when(i == 0)
    def _():
        cnt_scr[...] = jnp.zeros_like(cnt_scr)

    @pl.when(i < nbp)
    def _():
        ybuf[...] = yp_ref[...].astype(BF16)

    @pl.when(i >= nbp)
    def _():
        ybuf[...] = ys_ref[...].astype(BF16)

    acc = _dot(ybuf[...], w_ref[...])
    xn = _layer_norm(alpha * x_ref[...] + acc, ln_ref[0:1, :], ln_ref[1:2, :])
    o_ref[:, :d] = xn
    xh = xn.astype(BF16)
    xl = (xn - xh.astype(F32)).astype(BF16)
    hw = _dot(xh, wr_ref[...])
    logits = hw[:, :ROUTE_LANES] + hw[:, ROUTE_LANES:] + _dot(xl, wr_ref[:, :ROUTE_LANES]) + br_ref[...]
    tm = logits.shape[0]
    lane = _iota2((tm, ROUTE_LANES), 1)
    ninf = -jnp.inf
    gl = jnp.where(lane < n_groups, logits, ninf)
    gmax = jnp.max(gl, axis=-1, keepdims=True)
    gidx = jnp.min(jnp.where(gl == gmax, lane, ROUTE_LANES), axis=-1, keepdims=True)
    g_w = 1.0 / jnp.sum(jnp.where(lane < n_groups, jnp.exp(logits - gmax), 0.0), axis=-1, keepdims=True)
    lo = n_groups + epg * gidx
    el = jnp.where((lane >= lo) & (lane < lo + epg), logits, ninf)
    m1 = jnp.max(el, axis=-1, keepdims=True)
    i1 = jnp.min(jnp.where(el == m1, lane, ROUTE_LANES), axis=-1, keepdims=True)
    el2 = jnp.where(lane == i1, ninf, el)
    m2 = jnp.max(el2, axis=-1, keepdims=True)
    i2 = jnp.min(jnp.where(el2 == m2, lane, ROUTE_LANES), axis=-1, keepdims=True)
    r = jnp.exp(m2 - m1)
    w1 = g_w / (1.0 + r)
    w2 = g_w * r / (1.0 + r)
    e1 = i1 - lo
    e2 = i2 - lo
    first = e1 < e2
    ea = jnp.where(first, e1, e2)
    eb = jnp.where(first, e2, e1)
    bucket = gidx * PAIR_SLOTS + ea * epg + eb
    ga = jnp.where(first, w1, w2)
    gb = jnp.where(first, w2, w1)
    o_ref[:, d:] = jnp.where(lane == 1, ga, jnp.where(lane == 2, gb, 0.0))
    onehot = _iota2((tm, n_buckets), 1) == bucket
    oh = jnp.where(onehot, 1.0, 0.0)
    earlier = jnp.where(_iota2((tm, tm), 0) > _iota2((tm, tm), 1), 1.0, 0.0).astype(BF16)
    before = _dot(earlier, oh.astype(BF16)) + cnt_scr[...]
    rank = jnp.sum(jnp.where(onehot, before, 0.0), axis=-1, keepdims=True)
    cnt_scr[...] = cnt_scr[...] + jnp.sum(oh, axis=0, keepdims=True)
    bkt_ref[0] = _col_to_rows(bucket.astype(F32)).astype(jnp.int32)
    rank_ref[0] = _col_to_rows(rank).astype(jnp.int32)
    cnt_ref[...] = jnp.broadcast_to(cnt_scr[...], cnt_ref.shape).astype(jnp.int32)


def _outln(mixed_p, mixed_s, x, w, ln, wr, br, *, n, tm, alpha, n_groups, epg, n_buckets):
    d = x.shape[1]
    dm = mixed_p.shape[1]
    nbp = mixed_p.shape[0] // tm
    nb = n // tm
    return pl.pallas_call(
        functools.partial(_outln_kernel, alpha=alpha, n_groups=n_groups, epg=epg, nbp=nbp, n_buckets=n_buckets),
        grid=(nb,),
        in_specs=[pl.BlockSpec((tm, dm), lambda i: (jnp.minimum(i, nbp - 1), 0)),
                  pl.BlockSpec((tm, dm), lambda i: (jnp.maximum(i - nbp, 0), 0)),
                  pl.BlockSpec((tm, d), lambda i: (i, 0)),
                  pl.BlockSpec((dm, d), lambda i: (0, 0), pipeline_mode=pl.Buffered(1)),
                  pl.BlockSpec(ln.shape, lambda i: (0, 0)),
                  pl.BlockSpec(wr.shape, lambda i: (0, 0)),
                  pl.BlockSpec(br.shape, lambda i: (0, 0))],
        out_specs=(pl.BlockSpec((tm, d + ROUTE_LANES), lambda i: (i, 0)),
                   pl.BlockSpec((1, tm // LANE, LANE), lambda i: (i, 0, 0)),
                   pl.BlockSpec((1, tm // LANE, LANE), lambda i: (i, 0, 0)),
                   pl.BlockSpec((8, n_buckets), lambda i: (0, 0))),
        out_shape=(jax.ShapeDtypeStruct((n, d + ROUTE_LANES), F32),
                   jax.ShapeDtypeStruct((nb, tm // LANE, LANE), jnp.int32),
                   jax.ShapeDtypeStruct((nb, tm // LANE, LANE), jnp.int32),
                   jax.ShapeDtypeStruct((8, n_buckets), jnp.int32)),
        scratch_shapes=[pltpu.VMEM((tm, dm), BF16), pltpu.VMEM((1, n_buckets), F32)],
        compiler_params=pltpu.CompilerParams(dimension_semantics=("arbitrary",), vmem_limit_bytes=VMEM_LIMIT),
        name="outproj_ln_route")(mixed_p, mixed_s, x, w, ln, wr, br)


def _moe_kernel(c8_ref, ea_ref, eb_ref, src_ref, nxt_ref, x_hbm, w1a_ref, w3a_ref, w2a_ref, w1b_ref, w3b_ref,
                w2b_ref, ln_ref, o_hbm, xbuf, obuf, gsem, ssem, *, alpha, tr, n_rows):
    t = pl.program_id(0)
    nt = pl.num_programs(0)
    d = obuf.shape[2]
    slot = lax.rem(t, 2)
    other = 1 - slot
    c8_t = c8_ref[t]
    c8_next = jnp.where(t + 1 < nt, c8_ref[jnp.minimum(t + 1, nt - 1)], 0)

    ng = tr // DMA_ROWS

    def row_groups(c8, issue):
        def span(lo, hi):
            for g in range(lo, hi):
                @pl.when(g < c8)
                def _():
                    for u in range(DMA_ROWS):
                        issue(g * DMA_ROWS + u, u % 2)
        span(0, ng // 2)
        pl.when(c8 > ng // 2)(lambda: span(ng // 2, ng))

    def gather(idx_ref, s, c8):
        row_groups(c8, lambda r, prio: pltpu.make_async_copy(
            x_hbm.at[pl.ds(jnp.minimum(idx_ref[0, 0, r], n_rows - 1), 1), :],
            xbuf.at[s, pl.ds(r, 1), :], gsem.at[s]).start(priority=prio))

    def scatter(s, c8):
        row_groups(c8, lambda r, prio: pltpu.make_async_copy(
            obuf.at[s, pl.ds(r, 1), :], o_hbm.at[pl.ds(src_ref[0, 0, r], 1), :], ssem.at[s]).start(priority=prio))

    def wait_groups(desc, c8):
        def body(i, carry):
            desc.wait()
            return carry
        lax.fori_loop(0, c8, body, 0)

    def wait_gather(s, c8):
        wait_groups(pltpu.make_async_copy(x_hbm.at[pl.ds(0, DMA_ROWS), :], xbuf.at[s, pl.ds(0, DMA_ROWS), :],
                                          gsem.at[s]), c8)

    def wait_scatter(s, c8):
        wait_groups(pltpu.make_async_copy(obuf.at[s, pl.ds(0, DMA_ROWS), :], o_hbm.at[pl.ds(0, DMA_ROWS), :],
                                          ssem.at[s]), c8)

    @pl.when(t == 0)
    def _():
        xbuf[...] = jnp.zeros_like(xbuf)
        obuf[...] = jnp.zeros_like(obuf)
        for s in range(2):
            fill = pltpu.make_async_copy(obuf.at[s], o_hbm.at[pl.ds(n_rows + s * tr, tr), :], ssem.at[s])
            fill.start()
            fill.wait()
        gather(src_ref, 0, c8_t)

    gather(nxt_ref, other, c8_next)

    @pl.when(c8_t > 0)
    def _():
        wait_gather(slot, c8_t)
        x = xbuf[slot, :, :d]
        ga = xbuf[slot, :, d + 1:d + 2]
        gb = xbuf[slot, :, d + 2:d + 3]
        xb = x.astype(BF16)
        ha = _silu(_dot(xb, w1a_ref[0])) * _dot(xb, w3a_ref[0])
        hb = _silu(_dot(xb, w1b_ref[0])) * _dot(xb, w3b_ref[0])
        ffn = _dot((ha * ga).astype(BF16), w2a_ref[0]) + _dot((hb * gb).astype(BF16), w2b_ref[0])
        y = _layer_norm(alpha * x + ffn, ln_ref[0:1, :], ln_ref[1:2, :])

        @pl.when(t >= 2)
        def _():
            wait_scatter(slot, c8_ref[jnp.maximum(t - 2, 0)])

        obuf[slot] = y
        scatter(slot, c8_t)

        @pl.when(c8_next == 0)
        def _():
            wait_scatter(slot, c8_t)

            @pl.when(t >= 1)
            def _():
                wait_scatter(other, c8_ref[jnp.maximum(t - 1, 0)])


def _moe(x1r, src, c8, ea, eb, w1, w3, w2, ln, *, alpha, tr):
    n, dr = x1r.shape
    d = dr - ROUTE_LANES
    nt = c8.shape[0]
    f = w1.shape[2]
    wa = lambda t, v, a, b: (a[t], 0, 0)
    wb = lambda t, v, a, b: (b[t], 0, 0)
    grid_spec = pltpu.PrefetchScalarGridSpec(
        num_scalar_prefetch=3,
        grid=(nt,),
        in_specs=[pl.BlockSpec((1, 1, tr), lambda t, v, a, b: (t, 0, 0), memory_space=pltpu.SMEM),
                  pl.BlockSpec((1, 1, tr), lambda t, v, a, b: (jnp.minimum(t + 1, nt - 1), 0, 0),
                               memory_space=pltpu.SMEM),
                  pl.BlockSpec(memory_space=pl.ANY),
                  pl.BlockSpec((1, d, f), wa), pl.BlockSpec((1, d, f), wa), pl.BlockSpec((1, f, d), wa),
                  pl.BlockSpec((1, d, f), wb), pl.BlockSpec((1, d, f), wb), pl.BlockSpec((1, f, d), wb),
                  pl.BlockSpec(ln.shape, lambda t, v, a, b: (0, 0))],
        out_specs=pl.BlockSpec(memory_space=pl.ANY),
        scratch_shapes=[pltpu.VMEM((2, tr, dr), F32), pltpu.VMEM((2, tr, d), F32),
                        pltpu.SemaphoreType.DMA((2,)), pltpu.SemaphoreType.DMA((2,))])
    return pl.pallas_call(
        functools.partial(_moe_kernel, alpha=alpha, tr=tr, n_rows=n),
        grid_spec=grid_spec,
        out_shape=jax.ShapeDtypeStruct((n + 2 * tr, d), F32),
        compiler_params=pltpu.CompilerParams(dimension_semantics=("arbitrary",), vmem_limit_bytes=VMEM_LIMIT),
        name="moe_routed")(c8, ea, eb, src, src, x1r, w1, w3, w2, w1, w3, w2, ln)


def _route_schedule(bucket, rank, counts, *, n, tr, nt, epg, e_off):
    nbk = counts.shape[0]
    tiles_b = (counts + tr - 1) // tr
    tile_end = jnp.cumsum(tiles_b)
    pstart = (tile_end - tiles_b) * tr
    hit = bucket[:, None] == jnp.arange(nbk, dtype=jnp.int32)[None, :]
    dest = jnp.sum(jnp.where(hit, pstart[None, :], 0), axis=1) + rank
    slots = jnp.arange(nt * tr, dtype=jnp.int32)
    trash = n + ((slots // tr) % 2) * tr + slots % tr
    src = trash.at[dest].set(jnp.arange(n, dtype=jnp.int32))
    tiles = jnp.arange(nt, dtype=jnp.int32)
    tb = jnp.sum(tile_end[None, :] <= jnp.minimum(tiles, tile_end[-1] - 1)[:, None], axis=1).astype(jnp.int32)
    own = tb[:, None] == jnp.arange(nbk, dtype=jnp.int32)[None, :]
    pick = lambda tab: jnp.sum(jnp.where(own, tab[None, :], 0), axis=1)
    rows_t = jnp.clip(pick(counts) - (tiles - pick(tile_end - tiles_b)) * tr, 0, tr)
    c8 = jnp.where(tiles < tile_end[-1], (rows_t + DMA_ROWS - 1) // DMA_ROWS, 0)
    grp = tb // PAIR_SLOTS
    pair = tb % PAIR_SLOTS
    ea = e_off + grp * epg + pair // epg
    eb = e_off + grp * epg + pair % epg
    return src.reshape(nt, 1, tr), c8.astype(jnp.int32), ea.astype(jnp.int32), eb.astype(jnp.int32)


def _pad_cols(w, width):
    return jnp.pad(w, ((0, 0), (0, width - w.shape[1])))


def _pad_vec(v, width=LANE):
    return jnp.pad(v.astype(F32), (0, width - v.shape[0]))[None, :]


def _rows8(*vecs):
    m = jnp.stack([v.astype(F32) for v in vecs], axis=0)
    return jnp.pad(m, ((0, 8 - m.shape[0]), (0, 0)))


def _block_diag(w):
    nb, bd, _ = w.shape
    eye = jnp.eye(nb, dtype=w.dtype)
    return jnp.einsum("nde,nm->ndme", w, eye).reshape(nb * bd, nb * bd)


def kernel(x_prompt, x_sample, state_delta, state_delta_conv, state_lru, state_lru_conv, state_ssm, state_ssm_conv,
           meta, w_in_ab, dn_conv_w, dn_a_log, dn_dt_bias, dn_norm_g, lru_conv_w, lru_conv_b, lru_w_a, lru_b_a,
           lru_w_x, lru_b_x, lru_lam, w_out_ab, w_in_ssd, ssd_conv_w, ssd_conv_b, ssd_dt_bias, ssd_a_log, ssd_d,
           ssd_norm_g, w_out_ssd, ln_g, ln_b, moe_w_group, moe_b_group, moe_w_expert, moe_b_expert, moe_w1, moe_w3,
           moe_w2):
    bp, seq, d = x_prompt.shape
    bs, ls, _ = x_sample.shape
    depth = ln_g.shape[0]
    alpha = (2.0 * depth) ** 0.25
    n_groups, epg = moe_w1.shape[1], moe_w1.shape[2]
    heads_dn = dn_a_log.shape[1]
    nqk = heads_dn * state_delta.shape[3]
    nv = heads_dn * state_delta.shape[4]
    lru_w = lru_lam.shape[1]
    heads_ssd = ssd_a_log.shape[1]
    inner = heads_ssd * state_ssm.shape[3]
    n_state = state_ssm.shape[4]
    ssd_groups = (ssd_conv_w.shape[2] - inner) // (2 * n_state)
    assert nqk == nv and state_delta.shape[3] == LANE and state_delta.shape[4] == LANE and n_state == LANE

    lp_len = N_META + seq
    chunk_lcm = math.lcm(DN_BLOCK, SSD_CHUNK)
    lpad = -(-lp_len // chunk_lcm) * chunk_lcm
    np_rows = bp * lpad
    ns_rows = bs * ls
    ntot = np_rows + ns_rows
    tm = math.gcd(math.gcd(np_rows, ns_rows), ROW_TILE)
    assert tm % LANE == 0 and np_rows % ls == 0 and ls % 8 == 0 and ls >= HIST
    assert (lp_len - 1) % DN_BLOCK + 1 >= HIST and (lp_len - 1) % SSD_CHUNK + 1 >= HIST

    xp = jnp.concatenate([jnp.broadcast_to(meta.astype(F32), (bp, N_META, d)), x_prompt,
                          jnp.zeros((bp, lpad - lp_len, d), F32)], axis=1)
    x = jnp.concatenate([xp.reshape(np_rows, d), x_sample.reshape(ns_rows, d)], axis=0)

    tr = MOE_TILE
    n_buckets = n_groups * PAIR_SLOTS
    n_pairs = n_groups * (epg * (epg - 1) // 2)
    nt = -(-(ntot + n_pairs * (tr - 1)) // tr)
    n_exp = n_groups * epg
    w1 = moe_w1.reshape(depth * n_exp, d, -1).astype(BF16)
    w3 = moe_w3.reshape(depth * n_exp, d, -1).astype(BF16)
    w2 = moe_w2.reshape(depth * n_exp, -1, d).astype(BF16)

    zeros = lambda *s: jnp.zeros(s, F32)
    n_ab, n_ssd = (depth + 1) // 2, depth // 2
    st_ab_p = st_ab_s = st_ssd_p = st_ssd_s = None
    for layer in range(depth):
        i = layer // 2
        if layer % 2 == 0:
            wi = w_in_ab[i]
            o_b = 3 * nqk + nv
            w_in = jnp.concatenate([wi[:, :o_b], wi[:, o_b + 2 * heads_dn:],
                                    _pad_cols(wi[:, o_b:o_b + heads_dn], LANE),
                                    _pad_cols(wi[:, o_b + heads_dn:o_b + 2 * heads_dn], LANE)], axis=1).astype(BF16)
            proj = _inproj(x, w_in, tm, ntot)
            wts = (dn_conv_w[i], _rows8(_pad_vec(dn_a_log[i])[0], _pad_vec(dn_dt_bias[i])[0], dn_norm_g[i]),
                   lru_conv_w[i], _rows8(lru_conv_b[i], lru_b_a[i], lru_b_x[i], lru_lam[i]),
                   _block_diag(lru_w_a[i]).astype(BF16), _block_diag(lru_w_x[i]).astype(BF16))
            mixed_p, *st_ab_p = _ab_mixer(
                proj, zeros(1, bp, HIST, 3 * nqk), zeros(1, bp, *state_delta.shape[2:]), zeros(1, bp, HIST, lru_w),
                zeros(bp, 1, lru_w), wts, st_ab_p, t=DN_BLOCK, sub=DN_CHUNK, length=lp_len, batch=bp,
                cpb=lpad // DN_BLOCK, rb0=0, li=0, lo=i, n_l=n_ab)
            mixed_s, *st_ab_s = _ab_mixer(
                proj, state_delta_conv, state_delta, state_lru_conv, state_lru[i][:, None, :], wts, st_ab_s,
                t=ls, sub=ls, length=ls, batch=bs, cpb=1, rb0=np_rows // ls, li=i, lo=i, n_l=n_ab)
            w_out = w_out_ab[i].astype(BF16)
        else:
            wi = w_in_ssd[i]
            e_raw = wi.shape[1]
            w_in = _pad_cols(wi, -(-e_raw // LANE) * LANE).astype(BF16)
            proj = _inproj(x, w_in, tm, ntot)
            wts = (ssd_conv_w[i], ssd_conv_b[i][None, :],
                   _rows8(_pad_vec(ssd_dt_bias[i])[0], _pad_vec(ssd_a_log[i])[0], _pad_vec(ssd_d[i])[0]),
                   ssd_norm_g[i][None, :])
            mixed_p, *st_ssd_p = _ssd_mixer(
                proj, zeros(1, bp, HIST, ssd_conv_w.shape[2]), zeros(1, bp, *state_ssm.shape[2:]), wts, st_ssd_p,
                t=SSD_CHUNK, length=lp_len, batch=bp, cpb=lpad // SSD_CHUNK, rb0=0, groups=ssd_groups,
                li=0, lo=i, n_l=n_ssd)
            mixed_s, *st_ssd_s = _ssd_mixer(
                proj, state_ssm_conv, state_ssm, wts, st_ssd_s,
                t=ls, length=ls, batch=bs, cpb=1, rb0=np_rows // ls, groups=ssd_groups, li=i, lo=i, n_l=n_ssd)
            w_out = w_out_ssd[i].astype(BF16)
        wr = _pad_cols(jnp.concatenate([moe_w_group[layer], moe_w_expert[layer]], axis=1), ROUTE_LANES)
        wr_hi = wr.astype(BF16)
        wr = jnp.concatenate([wr_hi, (wr - wr_hi.astype(F32)).astype(BF16)], axis=1)
        br = _pad_vec(jnp.concatenate([moe_b_group[layer], moe_b_expert[layer]]), ROUTE_LANES)
        x1r, bucket, rank, counts = _outln(mixed_p, mixed_s, x, w_out, _rows8(ln_g[layer, 0], ln_b[layer, 0]), wr, br,
                                           n=ntot, tm=tm, alpha=alpha, n_groups=n_groups, epg=epg,
                                           n_buckets=n_buckets)
        src, c8, ea, eb = _route_schedule(bucket.reshape(ntot), rank.reshape(ntot), counts[0], n=ntot, tr=tr,
                                             nt=nt, epg=epg, e_off=layer * n_exp)
        x = _moe(x1r, src, c8, ea, eb, w1, w3, w2, _rows8(ln_g[layer, 1], ln_b[layer, 1]), alpha=alpha, tr=tr)

    y_prompt = x[:np_rows].reshape(bp, lpad, d)[:, N_META:lp_len]
    y_sample = x[np_rows:ntot].reshape(bs, ls, d)
    ab = lambda st: (st[0], st[1], st[2][:, :, 0], st[3])
    return (y_prompt, y_sample) + ab(st_ab_p) + tuple(st_ssd_p) + ab(st_ab_s) + tuple(st_ssd_s)
```
